```python
import jax, jax.numpy as jnp
from jax import lax
import numpy as np

D_MODEL = 2048
BATCH = 2
SEQ = 16384
DEPTH = 1

HEAD_DIM = 128
EPS = 1e-6
GDN_HEADS = 8
GDN_WIDTH = GDN_HEADS * HEAD_DIM
GDN_CONV = 4
GDN_CHUNK = 64
SWA_GROUPS = ((128, 1), (512, 4), (2048, 16))
SWA_HEADS_PER_GROUP = 4
SWA_HEADS = SWA_HEADS_PER_GROUP * len(SWA_GROUPS)
SWA_WIDTH = SWA_HEADS * HEAD_DIM
SWA_OUT_WIDTH = SWA_HEADS_PER_GROUP * HEAD_DIM
SWA_BLOCK = 128
ROPE_THETA = 10000.0
NEG_INF = -1e30
D_FF = 4 * D_MODEL
IN_SIZES = (3 * GDN_WIDTH, GDN_WIDTH, GDN_HEADS, GDN_HEADS, 3 * SWA_WIDTH, D_MODEL, D_MODEL)
IN_COLS = sum(IN_SIZES)

kernel_name = "hybrid_gdn_dilated_swa_block"


def _rmsnorm(x, w):
    x32 = x.astype(jnp.float32)
    y = x32 * lax.rsqrt(jnp.mean(jnp.square(x32), axis=-1, keepdims=True) + EPS)
    return (y * w.astype(jnp.float32)).astype(x.dtype)


def _l2norm(x):
    x32 = x.astype(jnp.float32)
    return x32 * lax.rsqrt(jnp.sum(jnp.square(x32), axis=-1, keepdims=True) + EPS)


def _split_cols(t):
    points, acc = [], 0
    for sz in IN_SIZES[:-1]:
        acc += sz
        points.append(acc)
    return jnp.split(t, points, axis=-1)


def _causal_conv(x, w):
    c = x.shape[-1]
    return lax.conv_general_dilated(
        x, w[:, None, :].astype(x.dtype), window_strides=(1,),
        padding=((GDN_CONV - 1, 0),), dimension_numbers=("NWC", "WIO", "NWC"),
        feature_group_count=c)


def _rope(x, positions):
    half = HEAD_DIM // 2
    inv_freq = ROPE_THETA ** (-jnp.arange(half, dtype=jnp.float32) / half)
    ang = positions.astype(jnp.float32)[..., None] * inv_freq
    cos = jnp.concatenate([jnp.cos(ang), jnp.cos(ang)], -1)[:, :, None, :]
    sin = jnp.concatenate([jnp.sin(ang), jnp.sin(ang)], -1)[:, :, None, :]
    x32 = x.astype(jnp.float32)
    rot = jnp.concatenate([-x32[..., half:], x32[..., :half]], -1)
    return (x32 * cos + rot * sin).astype(x.dtype)


def _gdn_chunked(q, k, v, g, beta):
    b, s, h, d = q.shape
    n = s // GDN_CHUNK

    def chunks(t):
        t = t.reshape((b, n, GDN_CHUNK, h) + t.shape[3:])
        return jnp.moveaxis(t, 3, 1)

    q, k, v, g, beta = (chunks(t) for t in (q, k, v, g, beta))
    G = jnp.cumsum(g, axis=-1)
    idx = jnp.arange(GDN_CHUNK)
    incl = idx[:, None] >= idx[None, :]
    strict = idx[:, None] > idx[None, :]
    diff = G[..., :, None] - G[..., None, :]
    decay = jnp.where(incl, jnp.exp(jnp.where(incl, diff, 0.0)), 0.0)
    k_beta = k * beta[..., None]
    a_kk = jnp.where(strict, jnp.einsum("bhnid,bhnjd->bhnij", k_beta, k) * decay, 0.0)
    eye = jnp.eye(GDN_CHUNK, dtype=q.dtype)
    rhs = jnp.concatenate([v * beta[..., None], k_beta * jnp.exp(G)[..., None]], axis=-1)
    sol = lax.linalg.triangular_solve(eye + a_kk, rhs, left_side=True, lower=True,
                                      unit_diagonal=True)
    u, w = sol[..., :d], sol[..., d:]
    a_qk = jnp.where(incl, jnp.einsum("bhnid,bhnjd->bhnij", q, k) * decay, 0.0)
    q_dec = q * jnp.exp(G)[..., None]
    g_last = G[..., -1]
    k_dec = k * jnp.exp(g_last[..., None] - G)[..., None]

    def step(state, inp):
        u_c, w_c, qd_c, aqk_c, kd_c, gl_c = inp
        v_new = u_c - jnp.einsum("bhcd,bhde->bhce", w_c, state)
        o_c = (jnp.einsum("bhcd,bhde->bhce", qd_c, state)
               + jnp.einsum("bhij,bhje->bhie", aqk_c, v_new))
        state = (state * jnp.exp(gl_c)[..., None, None]
                 + jnp.einsum("bhcd,bhce->bhde", kd_c, v_new))
        return state, o_c

    xs = tuple(jnp.moveaxis(t, 2, 0) for t in (u, w, q_dec, a_qk, k_dec, g_last))
    state0 = jnp.zeros((b, h, d, v.shape[-1]), q.dtype)
    _, o = lax.scan(step, state0, xs)
    o = jnp.moveaxis(o, 0, 2)
    return jnp.moveaxis(o, 1, 3).reshape(b, s, h, -1)


def _gdn_branch(qkv, z, a, bta, conv_w, a_log, dt_bias, norm_w):
    b, s, _ = qkv.shape
    qkv = jax.nn.silu(_causal_conv(qkv, conv_w))
    q, k, v = jnp.split(qkv, 3, axis=-1)
    q = _l2norm(q.reshape(b, s, GDN_HEADS, HEAD_DIM)) * (HEAD_DIM ** -0.5)
    k = _l2norm(k.reshape(b, s, GDN_HEADS, HEAD_DIM))
    v = v.reshape(b, s, GDN_HEADS, HEAD_DIM).astype(jnp.float32)
    beta = jax.nn.sigmoid(bta.astype(jnp.float32))
    g = -jnp.exp(a_log.astype(jnp.float32)) * jax.nn.softplus(
        a.astype(jnp.float32) + dt_bias.astype(jnp.float32))
    o = _gdn_chunked(q, k, v, g, beta)
    o = o * lax.rsqrt(jnp.mean(jnp.square(o), axis=-1, keepdims=True) + EPS)
    zh = z.reshape(b, s, GDN_HEADS, HEAD_DIM).astype(jnp.float32)
    o = o * norm_w.astype(jnp.float32) * jax.nn.silu(zh)
    return o.reshape(b, s, GDN_WIDTH).astype(qkv.dtype)


def _dilated_window_attention(q, k, v, n_back, dil):
    b, s, h, dh = q.shape
    L = s // dil
    nb = -(-L // SWA_BLOCK)
    Lp = nb * SWA_BLOCK

    def by_residue(t):
        t = t.reshape(b, L, dil, h, dh).transpose(0, 2, 1, 3, 4)
        t = jnp.pad(t, ((0, 0), (0, 0), (0, Lp - L), (0, 0), (0, 0)))
        return t.reshape(b, dil, nb, SWA_BLOCK, h, dh)

    def with_prev(t):
        prev = jnp.pad(t, ((0, 0), (0, 0), (1, 0), (0, 0), (0, 0), (0, 0)))[:, :, :-1]
        return jnp.concatenate([prev, t], axis=3)

    qb = by_residue(q)
    kk = with_prev(by_residue(k))
    vv = with_prev(by_residue(v))
    qi = jnp.arange(SWA_BLOCK)[:, None] + SWA_BLOCK
    kj = jnp.arange(2 * SWA_BLOCK)[None, :]
    dist = qi - kj
    blk = jnp.arange(nb)[:, None, None]
    valid = (dist >= 0) & (dist <= n_back) & ((blk - 1) * SWA_BLOCK + kj >= 0)
    scores = jnp.einsum("bdnqhc,bdnkhc->bdnhqk", qb, kk).astype(jnp.float32) * (HEAD_DIM ** -0.5)
    scores = jnp.where(valid[:, None], scores, NEG_INF)
    lse = jax.nn.logsumexp(scores, axis=-1)
    p = jnp.exp(scores - lse[..., None])
    o = jnp.einsum("bdnhqk,bdnkhc->bdnqhc", p.astype(v.dtype), vv)
    o = o.reshape(b, dil, Lp, h, dh)[:, :, :L].transpose(0, 2, 1, 3, 4).reshape(b, s, h, dh)
    lse = lse.transpose(0, 1, 2, 4, 3).reshape(b, dil, Lp, h)[:, :, :L]
    lse = lse.transpose(0, 2, 1, 3).reshape(b, s, h)
    return o, lse


def _swa_branch(qkv, positions, q_norm_w, k_norm_w):
    b, s, _ = qkv.shape
    q, k, v = jnp.split(qkv, 3, axis=-1)
    q = _rope(_rmsnorm(q.reshape(b, s, SWA_HEADS, HEAD_DIM), q_norm_w), positions)
    k = _rope(_rmsnorm(k.reshape(b, s, SWA_HEADS, HEAD_DIM), k_norm_w), positions)
    v = v.reshape(b, s, SWA_HEADS, HEAD_DIM)
    outs, lses = [], []
    for gi, (window, dil) in enumerate(SWA_GROUPS):
        hs = slice(gi * SWA_HEADS_PER_GROUP, (gi + 1) * SWA_HEADS_PER_GROUP)
        o, lse = _dilated_window_attention(q[:, :, hs], k[:, :, hs], v[:, :, hs], window // dil, dil)
        outs.append(o.astype(jnp.float32))
        lses.append(lse)
    wts = jax.nn.softmax(jnp.stack(lses), axis=0)
    o = jnp.sum(jnp.stack(outs) * wts[..., None], axis=0)
    return o.reshape(b, s, SWA_OUT_WIDTH).astype(qkv.dtype)


def setup_inputs(seed: int = 0) -> dict:
    key = jax.random.key(seed)
    ks = jax.random.split(key, 18)
    f32 = jnp.float32
    x = jax.random.normal(ks[0], (BATCH, SEQ, D_MODEL), f32)
    positions = jnp.broadcast_to(jnp.arange(SEQ, dtype=jnp.int32)[None, :], (BATCH, SEQ))
    ln1_w = 1.0 + 0.02 * jax.random.normal(ks[1], (DEPTH, D_MODEL), f32)
    w_in = jax.random.normal(ks[2], (DEPTH, D_MODEL, IN_COLS), f32) * D_MODEL ** -0.5
    gdn_conv_w = jax.random.normal(ks[3], (DEPTH, GDN_CONV, 3 * GDN_WIDTH), f32) * 0.5
    gdn_a_log = jnp.log(jax.random.uniform(ks[4], (DEPTH, GDN_HEADS), f32, 1.0, 16.0))
    dt = jnp.exp(jax.random.uniform(ks[5], (DEPTH, GDN_HEADS), f32, np.log(1e-3), np.log(1e-1)))
    gdn_dt_bias = dt + jnp.log(-jnp.expm1(-dt))
    gdn_norm_w = 1.0 + 0.02 * jax.random.normal(ks[6], (DEPTH, HEAD_DIM), f32)
    w_gdn_proj = jax.random.normal(ks[7], (DEPTH, GDN_WIDTH, D_MODEL), f32) * GDN_WIDTH ** -0.5
    swa_q_norm_w = 1.0 + 0.02 * jax.random.normal(ks[8], (DEPTH, HEAD_DIM), f32)
    swa_k_norm_w = 1.0 + 0.02 * jax.random.normal(ks[9], (DEPTH, HEAD_DIM), f32)
    w_swa_proj = jax.random.normal(ks[10], (DEPTH, SWA_OUT_WIDTH, D_MODEL), f32) * SWA_OUT_WIDTH ** -0.5
    w_out = jax.random.normal(ks[11], (DEPTH, D_MODEL, D_MODEL), f32) * D_MODEL ** -0.5
    ln2_w = 1.0 + 0.02 * jax.random.normal(ks[12], (DEPTH, D_MODEL), f32)
    w_ff1 = jax.random.normal(ks[13], (DEPTH, D_MODEL, D_FF), f32) * D_MODEL ** -0.5
    w_ff2 = jax.random.normal(ks[14], (DEPTH, D_FF, D_MODEL), f32) * D_FF ** -0.5
    return {"x": x, "positions": positions, "ln1_w": ln1_w, "w_in": w_in,
            "gdn_conv_w": gdn_conv_w, "gdn_a_log": gdn_a_log, "gdn_dt_bias": gdn_dt_bias,
            "gdn_norm_w": gdn_norm_w, "w_gdn_proj": w_gdn_proj,
            "swa_q_norm_w": swa_q_norm_w, "swa_k_norm_w": swa_k_norm_w,
            "w_swa_proj": w_swa_proj, "w_out": w_out, "ln2_w": ln2_w,
            "w_ff1": w_ff1, "w_ff2": w_ff2}


def reference(x, positions, ln1_w, w_in, gdn_conv_w, gdn_a_log, gdn_dt_bias, gdn_norm_w,
              w_gdn_proj, swa_q_norm_w, swa_k_norm_w, w_swa_proj, w_out, ln2_w, w_ff1, w_ff2):
    for l in range(DEPTH):
        h = _rmsnorm(x, ln1_w[l])
        proj = jnp.einsum("bsd,de->bse", h, w_in[l])
        gdn_qkv, gdn_z, gdn_a, gdn_b, swa_qkv, gate_a, gate_b = _split_cols(proj)
        o_a = _gdn_branch(gdn_qkv, gdn_z, gdn_a, gdn_b, gdn_conv_w[l], gdn_a_log[l],
                          gdn_dt_bias[l], gdn_norm_w[l])
        o_b = _swa_branch(swa_qkv, positions, swa_q_norm_w[l], swa_k_norm_w[l])
        y_a = jnp.einsum("bse,ed->bsd", o_a, w_gdn_proj[l])
        y_b = jnp.einsum("bse,ed->bsd", o_b, w_swa_proj[l])
        mixed = jax.nn.sigmoid(gate_a) * y_a + jax.nn.sigmoid(gate_b) * y_b
        x = x + jnp.einsum("bsd,de->bse", mixed, w_out[l])
        h = _rmsnorm(x, ln2_w[l])
        u = jax.nn.relu(jnp.einsum("bsd,df->bsf", h, w_ff1[l]))
        x = x + jnp.einsum("bsf,fd->bsd", jnp.square(u), w_ff2[l])
    return x
```

```python
import functools

import jax
import jax.numpy as jnp
from jax import lax
from jax.experimental import pallas as pl
from jax.experimental.pallas import tpu as pltpu

F32 = jnp.float32
BF16 = jnp.bfloat16

HEAD_DIM = 128
EPS = 1e-6
GDN_HEADS = 8
GDN_WIDTH = GDN_HEADS * HEAD_DIM
GDN_CONV = 4
GDN_CHUNK = 64
SWA_GROUPS = ((128, 1), (512, 4), (2048, 16))
SWA_HEADS_PER_GROUP = 4
SWA_GROUP_WIDTH = SWA_HEADS_PER_GROUP * HEAD_DIM
SWA_WIDTH = len(SWA_GROUPS) * SWA_GROUP_WIDTH
SWA_BLOCK = 128
ROPE_THETA = 10000.0
NEG_INF = -1e30

VMEM_LIMIT_BYTES = 52 * 1024 * 1024
LANES = 128
SUBLANES = 8
CONV_TAIL_ROWS = SUBLANES
NORM_ROWS = 256
GDN_BLOCK = 256
SWA_ROWS = 512
PROJ_TILE_N = 512


def _params(*sem):
    return pltpu.CompilerParams(dimension_semantics=sem, vmem_limit_bytes=VMEM_LIMIT_BYTES)


def _sigmoid(x):
    return jax.nn.sigmoid(x)


def _silu(x):
    return x * _sigmoid(x)


def _rope_table_kernel(pos_ref, invf_ref, cos_ref, sin_ref):
    ang = pos_ref[...].astype(F32) * invf_ref[...]
    cos_ref[...] = jnp.cos(ang)
    lane = lax.broadcasted_iota(jnp.int32, ang.shape, 1)
    s = jnp.sin(ang)
    sin_ref[...] = jnp.where(lane < HEAD_DIM // 2, -s, s)


def _rope_tables(pos_col, inv_freq2):
    t = pos_col.shape[0]
    tm = min(t, 2048)
    return pl.pallas_call(
        _rope_table_kernel,
        grid=(t // tm,),
        in_specs=[pl.BlockSpec((tm, 1), lambda i: (i, 0)),
                  pl.BlockSpec((1, HEAD_DIM), lambda i: (0, 0))],
        out_specs=[pl.BlockSpec((tm, HEAD_DIM), lambda i: (i, 0)),
                   pl.BlockSpec((tm, HEAD_DIM), lambda i: (i, 0))],
        out_shape=[jax.ShapeDtypeStruct((t, HEAD_DIM), F32)] * 2,
        compiler_params=_params("parallel"),
        name="rope_table",
    )(pos_col, inv_freq2)


def _rmsnorm_to(h_ref, x_ref, lnw_ref):
    tm = x_ref.shape[0]
    w = lnw_ref[...]

    def body(c, carry):
        rows = pl.ds(pl.multiple_of(c * NORM_ROWS, NORM_ROWS), NORM_ROWS)
        x = x_ref[rows, :]
        y = x * lax.rsqrt(jnp.mean(x * x, axis=-1, keepdims=True) + EPS)
        h_ref[rows, :] = (y * w).astype(BF16)
        return carry

    lax.fori_loop(0, tm // NORM_ROWS, body, 0)


def _in_proj_kernel(x_ref, lnw_ref, wab_ref, w_ref, ab_ref, *rest, tile_ranges):
    out_refs, h_ref = rest[:-1], rest[-1]
    j = pl.program_id(1)

    @pl.when(j == 0)
    def _():
        _rmsnorm_to(h_ref, x_ref, lnw_ref)
        ab_ref[...] = jnp.dot(h_ref[...], wab_ref[...], preferred_element_type=F32)

    acc = jnp.dot(h_ref[...], w_ref[...], preferred_element_type=F32)
    for (lo, hi), o_ref in zip(tile_ranges, out_refs):
        @pl.when((j >= lo) & (j < hi))
        def _(o_ref=o_ref):
            o_ref[...] = acc.astype(o_ref.dtype)


def _in_proj(x2, ln_w, w_cat, w_ab, widths, tm):
    t, k = x2.shape
    tn = PROJ_TILE_N
    ranges, lo = [], 0
    for wd in widths:
        ranges.append((lo, lo + wd // tn))
        lo += wd // tn
    n_tiles = lo

    def out_map(lo_, hi_):
        return lambda i, j: (i, jnp.clip(j - lo_, 0, hi_ - lo_ - 1))

    return pl.pallas_call(
        functools.partial(_in_proj_kernel, tile_ranges=tuple(ranges)),
        grid=(t // tm, n_tiles),
        in_specs=[pl.BlockSpec((tm, k), lambda i, j: (i, 0)),
                  pl.BlockSpec((1, k), lambda i, j: (0, 0)),
                  pl.BlockSpec((k, LANES), lambda i, j: (0, 0)),
                  pl.BlockSpec((k, tn), lambda i, j: (0, j))],
        out_specs=[pl.BlockSpec((tm, LANES), lambda i, j: (i, 0))]
        + [pl.BlockSpec((tm, tn), out_map(a, b)) for a, b in ranges],
        out_shape=[jax.ShapeDtypeStruct((t, LANES), F32)]
        + [jax.ShapeDtypeStruct((t, wd), BF16) for wd in widths],
        scratch_shapes=[pltpu.VMEM((tm, k), BF16)],
        compiler_params=_params("parallel", "arbitrary"),
        name="in_proj",
    )(x2, ln_w, w_ab, w_cat)


def _ff1_kernel(x_ref, lnw_ref, w_ref, o_ref, h_ref):
    @pl.when(pl.program_id(1) == 0)
    def _():
        _rmsnorm_to(h_ref, x_ref, lnw_ref)

    u = jnp.maximum(jnp.dot(h_ref[...], w_ref[...], preferred_element_type=F32), 0.0)
    o_ref[...] = (u * u).astype(o_ref.dtype)


def _ff1(x2, ln_w, w, tm, tn):
    t, k = x2.shape
    n = w.shape[1]
    return pl.pallas_call(
        _ff1_kernel,
        grid=(t // tm, n // tn),
        in_specs=[pl.BlockSpec((tm, k), lambda i, j: (i, 0)),
                  pl.BlockSpec((1, k), lambda i, j: (0, 0)),
                  pl.BlockSpec((k, tn), lambda i, j: (0, j))],
        out_specs=pl.BlockSpec((tm, tn), lambda i, j: (i, j)),
        out_shape=jax.ShapeDtypeStruct((t, n), BF16),
        scratch_shapes=[pltpu.VMEM((tm, k), BF16)],
        compiler_params=_params("parallel", "arbitrary"),
        name="ff1",
    )(x2, ln_w, w)


def _residual_matmul_kernel(a_ref, w_ref, r_ref, o_ref):
    o_ref[...] = r_ref[...] + jnp.dot(a_ref[...], w_ref[...], preferred_element_type=F32)


def _residual_matmul(a, w, resid, tm, tn, name):
    t, k = a.shape
    n = w.shape[1]
    return pl.pallas_call(
        _residual_matmul_kernel,
        grid=(t // tm, n // tn),
        in_specs=[pl.BlockSpec((tm, k), lambda i, j: (i, 0)),
                  pl.BlockSpec((k, tn), lambda i, j: (0, j)),
                  pl.BlockSpec((tm, tn), lambda i, j: (i, j))],
        out_specs=pl.BlockSpec((tm, tn), lambda i, j: (i, j)),
        out_shape=jax.ShapeDtypeStruct((t, n), F32),
        compiler_params=_params("parallel", "arbitrary"),
        name=name,
    )(a, w, resid)


def _gdn_kernel(qkvz_ref, ab_ref, convw_ref, alog_ref, dtb_ref, normw_ref, o_ref,
                xs_ref, act_ref, kt_ref, g_ref, gt_ref, beta_ref, state_ref):
    tb = qkvz_ref.shape[0]
    c_len, d, nh, w = GDN_CHUNK, HEAD_DIM, GDN_HEADS, GDN_WIDTH
    n_chunks = tb // c_len

    @pl.when(pl.program_id(1) == 0)
    def _():
        xs_ref[0:CONV_TAIL_ROWS, :] = jnp.zeros((CONV_TAIL_ROWS, 3 * w), F32)
        state_ref[...] = jnp.zeros(state_ref.shape, F32)

    xs_ref[CONV_TAIL_ROWS:CONV_TAIL_ROWS + tb, :] = qkvz_ref[:, 0:3 * w].astype(F32)
    for cg in range(3 * nh):
        cols = slice(cg * d, (cg + 1) * d)
        acc = xs_ref[CONV_TAIL_ROWS:CONV_TAIL_ROWS + tb, cols] * convw_ref[GDN_CONV - 1:GDN_CONV, cols]
        for s in range(1, GDN_CONV):
            acc = acc + (xs_ref[CONV_TAIL_ROWS - s:CONV_TAIL_ROWS - s + tb, cols]
                         * convw_ref[GDN_CONV - 1 - s:GDN_CONV - s, cols])
        y = _silu(acc)
        if cg < 2 * nh:
            y = y * lax.rsqrt(jnp.sum(y * y, axis=-1, keepdims=True) + EPS)
            if cg < nh:
                y = y * (d ** -0.5)
        act_ref[:, cols] = y
        if nh <= cg < 2 * nh:
            for c in range(n_chunks):
                kt_ref[c, cg - nh] = y[c * c_len:(c + 1) * c_len, :].T
    xs_ref[0:CONV_TAIL_ROWS, :] = xs_ref[tb:tb + CONV_TAIL_ROWS, :]

    ab = ab_ref[...]
    beta_ref[...] = _sigmoid(ab)
    z = ab + dtb_ref[...]
    softplus = jnp.maximum(z, 0.0) + jnp.log1p(jnp.exp(-jnp.abs(z)))
    g = -jnp.exp(alog_ref[...]) * softplus
    ri = lax.broadcasted_iota(jnp.int32, (tb, tb), 0)
    ci = lax.broadcasted_iota(jnp.int32, (tb, tb), 1)
    tri = ((ci <= ri) & (ri // c_len == ci // c_len)).astype(F32)
    g_cum = jnp.dot(tri, g, precision=lax.Precision.HIGHEST, preferred_element_type=F32)
    g_ref[...] = g_cum
    for c in range(n_chunks):
        gt_ref[c] = g_cum[c * c_len:(c + 1) * c_len, :].T

    ii = lax.broadcasted_iota(jnp.int32, (c_len, c_len), 0)
    jj = lax.broadcasted_iota(jnp.int32, (c_len, c_len), 1)
    incl = ii >= jj
    strict = ii > jj
    eye = (ii == jj).astype(F32)
    norm_w = normw_ref[...]

    def mm(a, b):
        return jnp.dot(a.astype(BF16), b.astype(BF16), preferred_element_type=F32)

    def chunk_body(c, carry):
        rows = pl.ds(pl.multiple_of(c * c_len, c_len), c_len)
        g_all = g_ref[rows, :]
        beta_all = beta_ref[rows, :]
        gt_all = gt_ref[c]
        for h in range(nh):
            hc = slice(h * d, (h + 1) * d)
            q = act_ref[rows, hc]
            k = act_ref[rows, w + h * d:w + (h + 1) * d]
            v = act_ref[rows, 2 * w + h * d:2 * w + (h + 1) * d]
            kt = kt_ref[c, h]
            g_col = g_all[:, h:h + 1]
            b_col = beta_all[:, nh + h:nh + h + 1]
            g_row = gt_all[h:h + 1, :]
            g_last = g_row[:, c_len - 1:c_len]
            diff = g_col - g_row
            decay = jnp.where(incl, jnp.exp(jnp.where(incl, diff, 0.0)), 0.0)
            e_g = jnp.exp(g_col)
            kb = k * b_col
            s1 = mm(jnp.concatenate([kb, q], axis=0), kt)
            a_kk = jnp.where(strict, s1[:c_len] * decay, 0.0)
            a_qk = jnp.where(incl, s1[c_len:] * decay, 0.0)
            p = eye - a_kk
            m = mm(a_kk, a_kk)
            for it in range(5):
                if it < 4:
                    pm = mm(jnp.concatenate([p, m], axis=0), m)
                    p = p + pm[:c_len]
                    m = pm[c_len:]
                else:
                    p = p + mm(p, m)
            uw = mm(p, jnp.concatenate([v * b_col, kb * e_g], axis=1))
            u, wmat = uw[:, :d], uw[:, d:]
            state = state_ref[h]
            ws = mm(jnp.concatenate([wmat, q * e_g], axis=0), state)
            v_new = u - ws[:c_len]
            o = ws[c_len:] + mm(a_qk, v_new)
            kdt = kt * jnp.exp(g_last - g_row)
            state_ref[h] = state * jnp.exp(g_last) + mm(kdt, v_new)
            o = o * lax.rsqrt(jnp.mean(o * o, axis=-1, keepdims=True) + EPS)
            zh = qkvz_ref[rows, 3 * w + h * d:3 * w + (h + 1) * d].astype(F32)
            o_ref[rows, hc] = (o * norm_w * _silu(zh)).astype(o_ref.dtype)
        return carry

    lax.fori_loop(0, n_chunks, chunk_body, 0)


def _gdn(qkvz, ab, conv_w, a_log, dt_bias, norm_w, batch, seq):
    tb = GDN_BLOCK
    nb = seq // tb
    w = GDN_WIDTH
    n_chunks = tb // GDN_CHUNK
    row = lambda b, s: (b * nb + s, 0)
    const = lambda b, s: (0, 0)
    return pl.pallas_call(
        _gdn_kernel,
        grid=(batch, nb),
        in_specs=[pl.BlockSpec((tb, 4 * w), row),
                  pl.BlockSpec((tb, LANES), row),
                  pl.BlockSpec((GDN_CONV, 3 * w), const),
                  pl.BlockSpec((1, LANES), const),
                  pl.BlockSpec((1, LANES), const),
                  pl.BlockSpec((1, HEAD_DIM), const)],
        out_specs=pl.BlockSpec((tb, w), row),
        out_shape=jax.ShapeDtypeStruct((batch * seq, w), BF16),
        scratch_shapes=[pltpu.VMEM((tb + CONV_TAIL_ROWS, 3 * w), F32),
                        pltpu.VMEM((tb, 3 * w), F32),
                        pltpu.VMEM((n_chunks, GDN_HEADS, HEAD_DIM, GDN_CHUNK), F32),
                        pltpu.VMEM((tb, LANES), F32),
                        pltpu.VMEM((n_chunks, LANES, GDN_CHUNK), F32),
                        pltpu.VMEM((tb, LANES), F32),
                        pltpu.VMEM((GDN_HEADS, HEAD_DIM, HEAD_DIM), F32)],
        compiler_params=_params("parallel", "arbitrary"),
        name="gdn",
    )(qkvz, ab, conv_w, a_log, dt_bias, norm_w)


def _swa_kernel(cur_ref, prev_ref, cos_ref, sin_ref, cosp_ref, sinp_ref, qnw_ref, knw_ref, o_ref, lse_ref):
    r_rows = cur_ref.shape[1]
    blk, d, gw = SWA_BLOCK, HEAD_DIM, SWA_GROUP_WIDTH
    first = pl.program_id(1) == 0
    qnw = qnw_ref[...]
    knw = knw_ref[...]
    qi = lax.broadcasted_iota(jnp.int32, (blk, blk), 0)
    kj = lax.broadcasted_iota(jnp.int32, (blk, blk), 1)
    mask_cur = kj <= qi
    mask_prev = kj >= qi

    def norm_rope(x, nw, cos, sin):
        x = x.astype(F32)
        x = x * lax.rsqrt(jnp.mean(x * x, axis=-1, keepdims=True) + EPS) * nw
        return x * cos + pltpu.roll(x, d // 2, 1) * sin

    def nt(a, b):
        return lax.dot_general(a, b, (((1,), (1,)), ((), ())), preferred_element_type=F32)

    for i in range(r_rows // blk):
        rows = slice(i * blk, (i + 1) * blk)
        cos_c, sin_c = cos_ref[0, rows, :], sin_ref[0, rows, :]
        if i == 0:
            cos_p, sin_p = cosp_ref[0], sinp_ref[0]
        else:
            prows = slice((i - 1) * blk, i * blk)
            cos_p, sin_p = cos_ref[0, prows, :], sin_ref[0, prows, :]
        for j in range(SWA_HEADS_PER_GROUP):
            qc, kc, vc = (slice(s * gw + j * d, s * gw + (j + 1) * d) for s in range(3))
            q = (norm_rope(cur_ref[0, rows, qc], qnw, cos_c, sin_c) * (d ** -0.5)).astype(BF16)
            k_cur = norm_rope(cur_ref[0, rows, kc], knw, cos_c, sin_c).astype(BF16)
            v_cur = cur_ref[0, rows, vc]
            if i == 0:
                k_prev = norm_rope(prev_ref[0, :, kc], knw, cos_p, sin_p).astype(BF16)
                v_prev = prev_ref[0, :, vc]
                m_prev = mask_prev & jnp.logical_not(first)
            else:
                k_prev = norm_rope(cur_ref[0, prows, kc], knw, cos_p, sin_p).astype(BF16)
                v_prev = cur_ref[0, prows, vc]
                m_prev = mask_prev
            s_cur = jnp.where(mask_cur, nt(q, k_cur), NEG_INF)
            s_prev = jnp.where(m_prev, nt(q, k_prev), NEG_INF)
            mx = jnp.maximum(jnp.max(s_cur, axis=-1, keepdims=True), jnp.max(s_prev, axis=-1, keepdims=True))
            p_cur = jnp.exp(s_cur - mx)
            p_prev = jnp.exp(s_prev - mx)
            den = jnp.sum(p_cur, axis=-1, keepdims=True) + jnp.sum(p_prev, axis=-1, keepdims=True)
            pv = (jnp.dot(p_cur.astype(BF16), v_cur, preferred_element_type=F32)
                  + jnp.dot(p_prev.astype(BF16), v_prev, preferred_element_type=F32))
            oc = slice(j * d, (j + 1) * d)
            o_ref[0, rows, oc] = pv / den
            lse_ref[0, rows, oc] = jnp.broadcast_to(mx + jnp.log(den), (blk, d))


def _swa_group(qkv_r, cos_r, sin_r, qnw, knw):
    ns, length, width = qkv_r.shape
    r_rows = min(SWA_ROWS, length)
    per = r_rows // SWA_BLOCK
    cur = lambda s, l: (s, l, 0)
    prev = lambda s, l: (s, jnp.maximum(l * per - 1, 0), 0)
    const = lambda s, l: (0, 0)
    out_sds = jax.ShapeDtypeStruct((ns, length, SWA_GROUP_WIDTH), F32)
    return pl.pallas_call(
        _swa_kernel,
        grid=(ns, length // r_rows),
        in_specs=[pl.BlockSpec((1, r_rows, width), cur),
                  pl.BlockSpec((1, SWA_BLOCK, width), prev),
                  pl.BlockSpec((1, r_rows, HEAD_DIM), cur),
                  pl.BlockSpec((1, r_rows, HEAD_DIM), cur),
                  pl.BlockSpec((1, SWA_BLOCK, HEAD_DIM), prev),
                  pl.BlockSpec((1, SWA_BLOCK, HEAD_DIM), prev),
                  pl.BlockSpec((1, HEAD_DIM), const),
                  pl.BlockSpec((1, HEAD_DIM), const)],
        out_specs=[pl.BlockSpec((1, r_rows, SWA_GROUP_WIDTH), cur)] * 2,
        out_shape=[out_sds, out_sds],
        compiler_params=_params("parallel", "arbitrary"),
        name="swa",
    )(qkv_r, qkv_r, cos_r, sin_r, cos_r, sin_r, qnw, knw)


def _mix_kernel(oa_ref, o0_ref, o1_ref, o2_ref, l0_ref, l1_ref, l2_ref, wg_ref, ws_ref, ga_ref, gb_ref, out_ref):
    l0, l1, l2 = l0_ref[...], l1_ref[...], l2_ref[...]
    mx = jnp.maximum(jnp.maximum(l0, l1), l2)
    e0, e1, e2 = jnp.exp(l0 - mx), jnp.exp(l1 - mx), jnp.exp(l2 - mx)
    ob = (o0_ref[...] * e0 + o1_ref[...] * e1 + o2_ref[...] * e2) / (e0 + e1 + e2)
    ya = jnp.dot(oa_ref[...], wg_ref[...], preferred_element_type=F32)
    yb = jnp.dot(ob.astype(BF16), ws_ref[...], preferred_element_type=F32)
    mixed = _sigmoid(ga_ref[...].astype(F32)) * ya + _sigmoid(gb_ref[...].astype(F32)) * yb
    out_ref[...] = mixed.astype(out_ref.dtype)


def _mix(oa, os_, ls_, wg, ws, gates, tm):
    t = oa.shape[0]
    n = wg.shape[1]
    row = lambda i: (i, 0)
    const = lambda i: (0, 0)
    gw = SWA_GROUP_WIDTH
    return pl.pallas_call(
        _mix_kernel,
        grid=(t // tm,),
        in_specs=[pl.BlockSpec((tm, oa.shape[1]), row)]
        + [pl.BlockSpec((tm, gw), row)] * 6
        + [pl.BlockSpec(wg.shape, const), pl.BlockSpec(ws.shape, const),
           pl.BlockSpec((tm, n), row), pl.BlockSpec((tm, n), lambda i: (i, 1))],
        out_specs=pl.BlockSpec((tm, n), row),
        out_shape=jax.ShapeDtypeStruct((t, n), BF16),
        compiler_params=_params("parallel"),
        name="mix",
    )(oa, *os_, *ls_, wg, ws, gates, gates)


def _to_residues(a, batch, seq, dil):
    wd = a.shape[-1]
    return a.reshape(batch, seq // dil, dil, wd).transpose(0, 2, 1, 3).reshape(batch * dil, seq // dil, wd)


def _from_residues(a, batch, seq, dil):
    wd = a.shape[-1]
    return a.reshape(batch, dil, seq // dil, wd).transpose(0, 2, 1, 3).reshape(batch * seq, wd)


def _layer(x2, positions, batch, seq, ln1_w, w_in, conv_w, a_log, dt_bias, gdn_norm_w, w_gdn_proj,
           q_norm_w, k_norm_w, w_swa_proj, w_out, ln2_w, w_ff1, w_ff2):
    d_model = x2.shape[1]
    gw3 = 3 * GDN_WIDTH
    o_z, o_a, o_b = gw3, gw3 + GDN_WIDTH, gw3 + GDN_WIDTH + GDN_HEADS
    o_swa = o_b + GDN_HEADS
    o_ga = o_swa + 3 * SWA_WIDTH
    cols = [w_in[:, :o_a]]
    for g in range(len(SWA_GROUPS)):
        for part in range(3):
            lo = o_swa + part * SWA_WIDTH + g * SWA_GROUP_WIDTH
            cols.append(w_in[:, lo:lo + SWA_GROUP_WIDTH])
    cols.append(w_in[:, o_ga:])
    w_cat = jnp.concatenate(cols, axis=1).astype(BF16)
    w_ab = jnp.pad(w_in[:, o_a:o_swa], ((0, 0), (0, LANES - 2 * GDN_HEADS))).astype(BF16)
    widths = (4 * GDN_WIDTH,) + (3 * SWA_GROUP_WIDTH,) * len(SWA_GROUPS) + (2 * d_model,)

    ab, qkvz, s0, s1, s2, gates = _in_proj(x2, ln1_w.reshape(1, -1), w_cat, w_ab, widths, tm=1024)

    pad8 = lambda v: jnp.pad(v.astype(F32), (0, LANES - v.shape[0])).reshape(1, LANES)
    o_a_branch = _gdn(qkvz, ab, conv_w.astype(F32), pad8(a_log), pad8(dt_bias),
                      gdn_norm_w.reshape(1, -1).astype(F32), batch, seq)

    half = HEAD_DIM // 2
    inv_freq = ROPE_THETA ** (-jnp.arange(half, dtype=F32) / half)
    inv_freq2 = jnp.concatenate([inv_freq, inv_freq]).reshape(1, HEAD_DIM)
    qnw = q_norm_w.reshape(1, -1).astype(F32)
    knw = k_norm_w.reshape(1, -1).astype(F32)
    outs, lses = [], []
    for (window, dil), sg in zip(SWA_GROUPS, (s0, s1, s2)):
        assert window // dil == SWA_BLOCK and seq % (dil * SWA_BLOCK) == 0
        pos_r = _to_residues(positions.reshape(-1, 1), batch, seq, dil).reshape(-1, 1)
        cos_t, sin_t = _rope_tables(pos_r, inv_freq2)
        shape3 = (batch * dil, seq // dil, HEAD_DIM)
        o_g, lse_g = _swa_group(_to_residues(sg, batch, seq, dil), cos_t.reshape(shape3), sin_t.reshape(shape3),
                                qnw, knw)
        outs.append(_from_residues(o_g, batch, seq, dil))
        lses.append(_from_residues(lse_g, batch, seq, dil))

    mixed = _mix(o_a_branch, outs, lses, w_gdn_proj.astype(BF16), w_swa_proj.astype(BF16), gates, tm=512)
    x2 = _residual_matmul(mixed, w_out.astype(BF16), x2, tm=512, tn=d_model, name="out_proj")
    u2 = _ff1(x2, ln2_w.reshape(1, -1), w_ff1.astype(BF16), tm=1024, tn=PROJ_TILE_N)
    return _residual_matmul(u2, w_ff2.astype(BF16), x2, tm=512, tn=PROJ_TILE_N, name="ff2")


def kernel(x, positions, ln1_w, w_in, gdn_conv_w, gdn_a_log, gdn_dt_bias, gdn_norm_w, w_gdn_proj,
           swa_q_norm_w, swa_k_norm_w, w_swa_proj, w_out, ln2_w, w_ff1, w_ff2):
    batch, seq, d_model = x.shape
    x2 = x.reshape(batch * seq, d_model)
    for l in range(ln1_w.shape[0]):
        x2 = _layer(x2, positions, batch, seq, ln1_w[l], w_in[l], gdn_conv_w[l], gdn_a_log[l], gdn_dt_bias[l],
                    gdn_norm_w[l], w_gdn_proj[l], swa_q_norm_w[l], swa_k_norm_w[l], w_swa_proj[l], w_out[l],
                    ln2_w[l], w_ff1[l], w_ff2[l])
    return x2.reshape(batch, seq, d_model)
```

```python
import functools

import jax
import jax.numpy as jnp
from jax import lax
from jax.experimental import pallas as pl
from jax.experimental.pallas import tpu as pltpu

F32 = jnp.float32
BF16 = jnp.bfloat16

HEAD_DIM = 128
EPS = 1e-6
GDN_HEADS = 8
GDN_WIDTH = GDN_HEADS * HEAD_DIM
GDN_CONV = 4
GDN_CHUNK = 64
SWA_GROUPS = ((128, 1), (512, 4), (2048, 16))
SWA_HEADS_PER_GROUP = 4
SWA_GROUP_WIDTH = SWA_HEADS_PER_GROUP * HEAD_DIM
SWA_WIDTH = len(SWA_GROUPS) * SWA_GROUP_WIDTH
SWA_BLOCK = 128
ROPE_THETA = 10000.0
NEG_INF = -1e30

VMEM_LIMIT_BYTES = 52 * 1024 * 1024
LANES = 128
SUBLANES = 8
CONV_TAIL_ROWS = SUBLANES
NORM_ROWS = 256
GDN_BLOCK = 256
SWA_ROWS = 512
PROJ_TILE_N = 512


def _params(*sem):
    return pltpu.CompilerParams(dimension_semantics=sem, vmem_limit_bytes=VMEM_LIMIT_BYTES)


def _sigmoid(x):
    return jax.nn.sigmoid(x)


def _silu(x):
    return x * _sigmoid(x)


def _rope_table_kernel(pos_ref, invf_ref, cos_ref, sin_ref):
    ang = pos_ref[...].astype(F32) * invf_ref[...]
    cos_ref[...] = jnp.cos(ang)
    lane = lax.broadcasted_iota(jnp.int32, ang.shape, 1)
    s = jnp.sin(ang)
    sin_ref[...] = jnp.where(lane < HEAD_DIM // 2, -s, s)


def _rope_tables(pos_col, inv_freq2):
    t = pos_col.shape[0]
    tm = min(t, 2048)
    return pl.pallas_call(
        _rope_table_kernel,
        grid=(t // tm,),
        in_specs=[pl.BlockSpec((tm, 1), lambda i: (i, 0)),
                  pl.BlockSpec((1, HEAD_DIM), lambda i: (0, 0))],
        out_specs=[pl.BlockSpec((tm, HEAD_DIM), lambda i: (i, 0)),
                   pl.BlockSpec((tm, HEAD_DIM), lambda i: (i, 0))],
        out_shape=[jax.ShapeDtypeStruct((t, HEAD_DIM), F32)] * 2,
        compiler_params=_params("parallel"),
        name="rope_table",
    )(pos_col, inv_freq2)


def _rmsnorm_to(h_ref, x_ref, lnw_ref):
    tm = x_ref.shape[0]
    w = lnw_ref[...]

    def body(c, carry):
        rows = pl.ds(pl.multiple_of(c * NORM_ROWS, NORM_ROWS), NORM_ROWS)
        x = x_ref[rows, :]
        y = x * lax.rsqrt(jnp.mean(x * x, axis=-1, keepdims=True) + EPS)
        h_ref[rows, :] = (y * w).astype(BF16)
        return carry

    lax.fori_loop(0, tm // NORM_ROWS, body, 0)


def _in_proj_kernel(x_ref, lnw_ref, wab_ref, w_ref, ab_ref, *rest, tile_ranges):
    out_refs, h_ref = rest[:-1], rest[-1]
    j = pl.program_id(1)

    @pl.when(j == 0)
    def _():
        _rmsnorm_to(h_ref, x_ref, lnw_ref)
        ab_ref[...] = jnp.dot(h_ref[...], wab_ref[...], preferred_element_type=F32)

    acc = jnp.dot(h_ref[...], w_ref[...], preferred_element_type=F32)
    for (lo, hi), o_ref in zip(tile_ranges, out_refs):
        @pl.when((j >= lo) & (j < hi))
        def _(o_ref=o_ref):
            o_ref[...] = acc.astype(o_ref.dtype)


def _in_proj(x2, ln_w, w_cat, w_ab, widths, tm):
    t, k = x2.shape
    tn = PROJ_TILE_N
    ranges, lo = [], 0
    for wd in widths:
        ranges.append((lo, lo + wd // tn))
        lo += wd // tn
    n_tiles = lo

    def out_map(lo_, hi_):
        return lambda i, j: (i, jnp.clip(j - lo_, 0, hi_ - lo_ - 1))

    return pl.pallas_call(
        functools.partial(_in_proj_kernel, tile_ranges=tuple(ranges)),
        grid=(t // tm, n_tiles),
        in_specs=[pl.BlockSpec((tm, k), lambda i, j: (i, 0)),
                  pl.BlockSpec((1, k), lambda i, j: (0, 0)),
                  pl.BlockSpec((k, LANES), lambda i, j: (0, 0)),
                  pl.BlockSpec((k, tn), lambda i, j: (0, j))],
        out_specs=[pl.BlockSpec((tm, LANES), lambda i, j: (i, 0))]
        + [pl.BlockSpec((tm, tn), out_map(a, b)) for a, b in ranges],
        out_shape=[jax.ShapeDtypeStruct((t, LANES), F32)]
        + [jax.ShapeDtypeStruct((t, wd), BF16) for wd in widths],
        scratch_shapes=[pltpu.VMEM((tm, k), BF16)],
        compiler_params=_params("parallel", "arbitrary"),
        name="in_proj",
    )(x2, ln_w, w_ab, w_cat)


def _ff1_kernel(x_ref, lnw_ref, w_ref, o_ref, h_ref):
    @pl.when(pl.program_id(1) == 0)
    def _():
        _rmsnorm_to(h_ref, x_ref, lnw_ref)

    u = jnp.maximum(jnp.dot(h_ref[...], w_ref[...], preferred_element_type=F32), 0.0)
    o_ref[...] = (u * u).astype(o_ref.dtype)


def _ff1(x2, ln_w, w, tm, tn):
    t, k = x2.shape
    n = w.shape[1]
    return pl.pallas_call(
        _ff1_kernel,
        grid=(t // tm, n // tn),
        in_specs=[pl.BlockSpec((tm, k), lambda i, j: (i, 0)),
                  pl.BlockSpec((1, k), lambda i, j: (0, 0)),
                  pl.BlockSpec((k, tn), lambda i, j: (0, j))],
        out_specs=pl.BlockSpec((tm, tn), lambda i, j: (i, j)),
        out_shape=jax.ShapeDtypeStruct((t, n), BF16),
        scratch_shapes=[pltpu.VMEM((tm, k), BF16)],
        compiler_params=_params("parallel", "arbitrary"),
        name="ff1",
    )(x2, ln_w, w)


def _residual_matmul_kernel(a_ref, w_ref, r_ref, o_ref):
    o_ref[...] = r_ref[...] + jnp.dot(a_ref[...], w_ref[...], preferred_element_type=F32)


def _residual_matmul(a, w, resid, tm, tn, name):
    t, k = a.shape
    n = w.shape[1]
    return pl.pallas_call(
        _residual_matmul_kernel,
        grid=(t // tm, n // tn),
        in_specs=[pl.BlockSpec((tm, k), lambda i, j: (i, 0)),
                  pl.BlockSpec((k, tn), lambda i, j: (0, j)),
                  pl.BlockSpec((tm, tn), lambda i, j: (i, j))],
        out_specs=pl.BlockSpec((tm, tn), lambda i, j: (i, j)),
        out_shape=jax.ShapeDtypeStruct((t, n), F32),
        compiler_params=_params("parallel", "arbitrary"),
        name=name,
    )(a, w, resid)


def _gdn_kernel(qkvz_ref, ab_ref, convw_ref, alog_ref, dtb_ref, normw_ref, o_ref,
                xs_ref, act_ref, kt_ref, g_ref, gt_ref, beta_ref, state_ref):
    nbatch, tb = qkvz_ref.shape[0], qkvz_ref.shape[1]
    c_len, d, nh, w = GDN_CHUNK, HEAD_DIM, GDN_HEADS, GDN_WIDTH
    n_chunks = tb // c_len

    @pl.when(pl.program_id(0) == 0)
    def _():
        xs_ref[:, 0:CONV_TAIL_ROWS, :] = jnp.zeros((nbatch, CONV_TAIL_ROWS, 3 * w), F32)
        state_ref[...] = jnp.zeros(state_ref.shape, F32)

    ri = lax.broadcasted_iota(jnp.int32, (tb, tb), 0)
    ci = lax.broadcasted_iota(jnp.int32, (tb, tb), 1)
    tri = ((ci <= ri) & (ri // c_len == ci // c_len)).astype(F32)

    for b in range(nbatch):
        xs_ref[b, CONV_TAIL_ROWS:CONV_TAIL_ROWS + tb, :] = qkvz_ref[b, :, 0:3 * w].astype(F32)
        for cg in range(3 * nh):
            cols = slice(cg * d, (cg + 1) * d)
            acc = (xs_ref[b, CONV_TAIL_ROWS:CONV_TAIL_ROWS + tb, cols]
                   * convw_ref[GDN_CONV - 1:GDN_CONV, cols])
            for s in range(1, GDN_CONV):
                acc = acc + (xs_ref[b, CONV_TAIL_ROWS - s:CONV_TAIL_ROWS - s + tb, cols]
                             * convw_ref[GDN_CONV - 1 - s:GDN_CONV - s, cols])
            y = _silu(acc)
            if cg < 2 * nh:
                y = y * lax.rsqrt(jnp.sum(y * y, axis=-1, keepdims=True) + EPS)
                if cg < nh:
                    y = y * (d ** -0.5)
            act_ref[b, :, cols] = y
            if nh <= cg < 2 * nh:
                for c in range(n_chunks):
                    kt_ref[b, c, cg - nh] = y[c * c_len:(c + 1) * c_len, :].T
        xs_ref[b, 0:CONV_TAIL_ROWS, :] = xs_ref[b, tb:tb + CONV_TAIL_ROWS, :]

        ab = ab_ref[b]
        beta_ref[b] = _sigmoid(ab)
        z = ab + dtb_ref[...]
        softplus = jnp.maximum(z, 0.0) + jnp.log1p(jnp.exp(-jnp.abs(z)))
        g = -jnp.exp(alog_ref[...]) * softplus
        g_cum = jnp.dot(tri, g, precision=lax.Precision.HIGHEST, preferred_element_type=F32)
        g_ref[b] = g_cum
        for c in range(n_chunks):
            gt_ref[b, c] = g_cum[c * c_len:(c + 1) * c_len, :].T

    ii = lax.broadcasted_iota(jnp.int32, (c_len, c_len), 0)
    jj = lax.broadcasted_iota(jnp.int32, (c_len, c_len), 1)
    incl = ii >= jj
    strict = ii > jj
    norm_w = normw_ref[...]
    chains = [(b, h) for b in range(nbatch) for h in range(nh)]

    def mm(a, b):
        return jnp.dot(a.astype(BF16), b.astype(BF16), preferred_element_type=F32)

    def chunk_body(c, carry):
        rows = pl.ds(pl.multiple_of(c * c_len, c_len), c_len)
        g_all = [g_ref[b, rows, :] for b in range(nbatch)]
        beta_all = [beta_ref[b, rows, :] for b in range(nbatch)]
        gt_all = [gt_ref[b, c] for b in range(nbatch)]

        st = []
        for b, h in chains:
            q = act_ref[b, rows, h * d:(h + 1) * d]
            k = act_ref[b, rows, w + h * d:w + (h + 1) * d]
            v = act_ref[b, rows, 2 * w + h * d:2 * w + (h + 1) * d]
            kt = kt_ref[b, c, h]
            g_col = g_all[b][:, h:h + 1]
            b_col = beta_all[b][:, nh + h:nh + h + 1]
            g_row = gt_all[b][h:h + 1, :]
            g_last = g_row[:, c_len - 1:c_len]
            decay = jnp.where(incl, jnp.exp(jnp.where(incl, g_col - g_row, 0.0)), 0.0)
            e_g = jnp.exp(g_col)
            kb = k * b_col
            st.append(dict(q=q, kt=kt, g_row=g_row, g_last=g_last, decay=decay, kb=kb,
                           qe=q * e_g, rhs=jnp.concatenate([v * b_col, kb * e_g], axis=1)))
        for s in st:
            s1 = mm(jnp.concatenate([s["kb"], s["q"]], axis=0), s["kt"])
            s["a"] = jnp.where(strict, s1[:c_len] * s["decay"], 0.0)
            s["a_qk"] = jnp.where(incl, s1[c_len:] * s["decay"], 0.0)
        for s in st:
            s["x"] = s["rhs"] - mm(s["a"], s["rhs"])
            s["m"] = mm(s["a"], s["a"])
        for it in range(5):
            for s in st:
                s["x"] = s["x"] + mm(s["m"], s["x"])
                if it < 4:
                    s["m"] = mm(s["m"], s["m"])
        for (b, h), s in zip(chains, st):
            s["state"] = state_ref[b, h]
            s["ws"] = mm(jnp.concatenate([s["x"][:, d:], s["qe"]], axis=0), s["state"])
        for (b, h), s in zip(chains, st):
            v_new = s["x"][:, :d] - s["ws"][:c_len]
            o = s["ws"][c_len:] + mm(s["a_qk"], v_new)
            kdt = s["kt"] * jnp.exp(s["g_last"] - s["g_row"])
            state_ref[b, h] = s["state"] * jnp.exp(s["g_last"]) + mm(kdt, v_new)
            o = o * lax.rsqrt(jnp.mean(o * o, axis=-1, keepdims=True) + EPS)
            zh = qkvz_ref[b, rows, 3 * w + h * d:3 * w + (h + 1) * d].astype(F32)
            o_ref[b, rows, h * d:(h + 1) * d] = (o * norm_w * _silu(zh)).astype(o_ref.dtype)
        return carry

    lax.fori_loop(0, n_chunks, chunk_body, 0)


def _gdn(qkvz, ab, conv_w, a_log, dt_bias, norm_w, batch, seq):
    tb = GDN_BLOCK
    w = GDN_WIDTH
    n_chunks = tb // GDN_CHUNK
    blk = lambda s: (0, s, 0)
    const = lambda s: (0, 0)
    out = pl.pallas_call(
        _gdn_kernel,
        grid=(seq // tb,),
        in_specs=[pl.BlockSpec((batch, tb, 4 * w), blk),
                  pl.BlockSpec((batch, tb, LANES), blk),
                  pl.BlockSpec((GDN_CONV, 3 * w), const),
                  pl.BlockSpec((1, LANES), const),
                  pl.BlockSpec((1, LANES), const),
                  pl.BlockSpec((1, HEAD_DIM), const)],
        out_specs=pl.BlockSpec((batch, tb, w), blk),
        out_shape=jax.ShapeDtypeStruct((batch, seq, w), BF16),
        scratch_shapes=[pltpu.VMEM((batch, tb + CONV_TAIL_ROWS, 3 * w), F32),
                        pltpu.VMEM((batch, tb, 3 * w), F32),
                        pltpu.VMEM((batch, n_chunks, GDN_HEADS, HEAD_DIM, GDN_CHUNK), F32),
                        pltpu.VMEM((batch, tb, LANES), F32),
                        pltpu.VMEM((batch, n_chunks, LANES, GDN_CHUNK), F32),
                        pltpu.VMEM((batch, tb, LANES), F32),
                        pltpu.VMEM((batch, GDN_HEADS, HEAD_DIM, HEAD_DIM), F32)],
        compiler_params=_params("arbitrary"),
        name="gdn",
    )(qkvz.reshape(batch, seq, 4 * w), ab.reshape(batch, seq, LANES), conv_w, a_log, dt_bias, norm_w)
    return out.reshape(batch * seq, w)


def _swa_kernel(cur_ref, prev_ref, cos_ref, sin_ref, cosp_ref, sinp_ref, qnw_ref, knw_ref, o_ref, lse_ref):
    r_rows = cur_ref.shape[1]
    blk, d, gw = SWA_BLOCK, HEAD_DIM, SWA_GROUP_WIDTH
    first = pl.program_id(1) == 0
    qnw = qnw_ref[...]
    knw = knw_ref[...]
    qi = lax.broadcasted_iota(jnp.int32, (blk, blk), 0)
    kj = lax.broadcasted_iota(jnp.int32, (blk, blk), 1)
    mask_cur = kj <= qi
    mask_prev = kj >= qi

    def norm_rope(x, nw, cos, sin):
        x = x.astype(F32)
        x = x * lax.rsqrt(jnp.mean(x * x, axis=-1, keepdims=True) + EPS) * nw
        return x * cos + pltpu.roll(x, d // 2, 1) * sin

    def nt(a, b):
        return lax.dot_general(a, b, (((1,), (1,)), ((), ())), preferred_element_type=F32)

    for i in range(r_rows // blk):
        rows = slice(i * blk, (i + 1) * blk)
        cos_c, sin_c = cos_ref[0, rows, :], sin_ref[0, rows, :]
        if i == 0:
            cos_p, sin_p = cosp_ref[0], sinp_ref[0]
        else:
            prows = slice((i - 1) * blk, i * blk)
            cos_p, sin_p = cos_ref[0, prows, :], sin_ref[0, prows, :]
        for j in range(SWA_HEADS_PER_GROUP):
            qc, kc, vc = (slice(s * gw + j * d, s * gw + (j + 1) * d) for s in range(3))
            q = (norm_rope(cur_ref[0, rows, qc], qnw, cos_c, sin_c) * (d ** -0.5)).astype(BF16)
            k_cur = norm_rope(cur_ref[0, rows, kc], knw, cos_c, sin_c).astype(BF16)
            v_cur = cur_ref[0, rows, vc]
            if i == 0:
                k_prev = norm_rope(prev_ref[0, :, kc], knw, cos_p, sin_p).astype(BF16)
                v_prev = prev_ref[0, :, vc]
                m_prev = mask_prev & jnp.logical_not(first)
            else:
                k_prev = norm_rope(cur_ref[0, prows, kc], knw, cos_p, sin_p).astype(BF16)
                v_prev = cur_ref[0, prows, vc]
                m_prev = mask_prev
            s_cur = jnp.where(mask_cur, nt(q, k_cur), NEG_INF)
            s_prev = jnp.where(m_prev, nt(q, k_prev), NEG_INF)
            mx = jnp.maximum(jnp.max(s_cur, axis=-1, keepdims=True), jnp.max(s_prev, axis=-1, keepdims=True))
            p_cur = jnp.exp(s_cur - mx)
            p_prev = jnp.exp(s_prev - mx)
            den = jnp.sum(p_cur, axis=-1, keepdims=True) + jnp.sum(p_prev, axis=-1, keepdims=True)
            pv = (jnp.dot(p_cur.astype(BF16), v_cur, preferred_element_type=F32)
                  + jnp.dot(p_prev.astype(BF16), v_prev, preferred_element_type=F32))
            oc = slice(j * d, (j + 1) * d)
            o_ref[0, rows, oc] = pv / den
            lse_ref[0, rows, oc] = jnp.broadcast_to(mx + jnp.log(den), (blk, d))


def _swa_group(qkv_r, cos_r, sin_r, qnw, knw):
    ns, length, width = qkv_r.shape
    r_rows = min(SWA_ROWS, length)
    per = r_rows // SWA_BLOCK
    cur = lambda s, l: (s, l, 0)
    prev = lambda s, l: (s, jnp.maximum(l * per - 1, 0), 0)
    const = lambda s, l: (0, 0)
    out_sds = jax.ShapeDtypeStruct((ns, length, SWA_GROUP_WIDTH), F32)
    return pl.pallas_call(
        _swa_kernel,
        grid=(ns, length // r_rows),
        in_specs=[pl.BlockSpec((1, r_rows, width), cur),
                  pl.BlockSpec((1, SWA_BLOCK, width), prev),
                  pl.BlockSpec((1, r_rows, HEAD_DIM), cur),
                  pl.BlockSpec((1, r_rows, HEAD_DIM), cur),
                  pl.BlockSpec((1, SWA_BLOCK, HEAD_DIM), prev),
                  pl.BlockSpec((1, SWA_BLOCK, HEAD_DIM), prev),
                  pl.BlockSpec((1, HEAD_DIM), const),
                  pl.BlockSpec((1, HEAD_DIM), const)],
        out_specs=[pl.BlockSpec((1, r_rows, SWA_GROUP_WIDTH), cur)] * 2,
        out_shape=[out_sds, out_sds],
        compiler_params=_params("parallel", "arbitrary"),
        name="swa",
    )(qkv_r, qkv_r, cos_r, sin_r, cos_r, sin_r, qnw, knw)


def _mix_kernel(oa_ref, o0_ref, o1_ref, o2_ref, l0_ref, l1_ref, l2_ref, wg_ref, ws_ref, ga_ref, gb_ref, out_ref):
    l0, l1, l2 = l0_ref[...], l1_ref[...], l2_ref[...]
    mx = jnp.maximum(jnp.maximum(l0, l1), l2)
    e0, e1, e2 = jnp.exp(l0 - mx), jnp.exp(l1 - mx), jnp.exp(l2 - mx)
    ob = (o0_ref[...] * e0 + o1_ref[...] * e1 + o2_ref[...] * e2) / (e0 + e1 + e2)
    ya = jnp.dot(oa_ref[...], wg_ref[...], preferred_element_type=F32)
    yb = jnp.dot(ob.astype(BF16), ws_ref[...], preferred_element_type=F32)
    mixed = _sigmoid(ga_ref[...].astype(F32)) * ya + _sigmoid(gb_ref[...].astype(F32)) * yb
    out_ref[...] = mixed.astype(out_ref.dtype)


def _mix(oa, os_, ls_, wg, ws, gates, tm):
    t = oa.shape[0]
    n = wg.shape[1]
    row = lambda i: (i, 0)
    const = lambda i: (0, 0)
    gw = SWA_GROUP_WIDTH
    return pl.pallas_call(
        _mix_kernel,
        grid=(t // tm,),
        in_specs=[pl.BlockSpec((tm, oa.shape[1]), row)]
        + [pl.BlockSpec((tm, gw), row)] * 6
        + [pl.BlockSpec(wg.shape, const), pl.BlockSpec(ws.shape, const),
           pl.BlockSpec((tm, n), row), pl.BlockSpec((tm, n), lambda i: (i, 1))],
        out_specs=pl.BlockSpec((tm, n), row),
        out_shape=jax.ShapeDtypeStruct((t, n), BF16),
        compiler_params=_params("parallel"),
        name="mix",
    )(oa, *os_, *ls_, wg, ws, gates, gates)


def _to_residues(a, batch, seq, dil):
    wd = a.shape[-1]
    return a.reshape(batch, seq // dil, dil, wd).transpose(0, 2, 1, 3).reshape(batch * dil, seq // dil, wd)


def _from_residues(a, batch, seq, dil):
    wd = a.shape[-1]
    return a.reshape(batch, dil, seq // dil, wd).transpose(0, 2, 1, 3).reshape(batch * seq, wd)


def _layer(x2, positions, batch, seq, ln1_w, w_in, conv_w, a_log, dt_bias, gdn_norm_w, w_gdn_proj,
           q_norm_w, k_norm_w, w_swa_proj, w_out, ln2_w, w_ff1, w_ff2):
    d_model = x2.shape[1]
    gw3 = 3 * GDN_WIDTH
    o_z, o_a, o_b = gw3, gw3 + GDN_WIDTH, gw3 + GDN_WIDTH + GDN_HEADS
    o_swa = o_b + GDN_HEADS
    o_ga = o_swa + 3 * SWA_WIDTH
    cols = [w_in[:, :o_a]]
    for g in range(len(SWA_GROUPS)):
        for part in range(3):
            lo = o_swa + part * SWA_WIDTH + g * SWA_GROUP_WIDTH
            cols.append(w_in[:, lo:lo + SWA_GROUP_WIDTH])
    cols.append(w_in[:, o_ga:])
    w_cat = jnp.concatenate(cols, axis=1).astype(BF16)
    w_ab = jnp.pad(w_in[:, o_a:o_swa], ((0, 0), (0, LANES - 2 * GDN_HEADS))).astype(BF16)
    widths = (4 * GDN_WIDTH,) + (3 * SWA_GROUP_WIDTH,) * len(SWA_GROUPS) + (2 * d_model,)

    ab, qkvz, s0, s1, s2, gates = _in_proj(x2, ln1_w.reshape(1, -1), w_cat, w_ab, widths, tm=1024)

    pad8 = lambda v: jnp.pad(v.astype(F32), (0, LANES - v.shape[0])).reshape(1, LANES)
    o_a_branch = _gdn(qkvz, ab, conv_w.astype(F32), pad8(a_log), pad8(dt_bias),
                      gdn_norm_w.reshape(1, -1).astype(F32), batch, seq)

    half = HEAD_DIM // 2
    inv_freq = ROPE_THETA ** (-jnp.arange(half, dtype=F32) / half)
    inv_freq2 = jnp.concatenate([inv_freq, inv_freq]).reshape(1, HEAD_DIM)
    qnw = q_norm_w.reshape(1, -1).astype(F32)
    knw = k_norm_w.reshape(1, -1).astype(F32)
    outs, lses = [], []
    for (window, dil), sg in zip(SWA_GROUPS, (s0, s1, s2)):
        assert window // dil == SWA_BLOCK and seq % (dil * SWA_BLOCK) == 0
        pos_r = _to_residues(positions.reshape(-1, 1), batch, seq, dil).reshape(-1, 1)
        cos_t, sin_t = _rope_tables(pos_r, inv_freq2)
        shape3 = (batch * dil, seq // dil, HEAD_DIM)
        o_g, lse_g = _swa_group(_to_residues(sg, batch, seq, dil), cos_t.reshape(shape3), sin_t.reshape(shape3),
                                qnw, knw)
        outs.append(_from_residues(o_g, batch, seq, dil))
        lses.append(_from_residues(lse_g, batch, seq, dil))

    mixed = _mix(o_a_branch, outs, lses, w_gdn_proj.astype(BF16), w_swa_proj.astype(BF16), gates, tm=512)
    x2 = _residual_matmul(mixed, w_out.astype(BF16), x2, tm=512, tn=d_model, name="out_proj")
    u2 = _ff1(x2, ln2_w.reshape(1, -1), w_ff1.astype(BF16), tm=1024, tn=PROJ_TILE_N)
    return _residual_matmul(u2, w_ff2.astype(BF16), x2, tm=512, tn=PROJ_TILE_N, name="ff2")


def kernel(x, positions, ln1_w, w_in, gdn_conv_w, gdn_a_log, gdn_dt_bias, gdn_norm_w, w_gdn_proj,
           swa_q_norm_w, swa_k_norm_w, w_swa_proj, w_out, ln2_w, w_ff1, w_ff2):
    batch, seq, d_model = x.shape
    x2 = x.reshape(batch * seq, d_model)
    for l in range(ln1_w.shape[0]):
        x2 = _layer(x2, positions, batch, seq, ln1_w[l], w_in[l], gdn_conv_w[l], gdn_a_log[l], gdn_dt_bias[l],
                    gdn_norm_w[l], w_gdn_proj[l], swa_q_norm_w[l], swa_k_norm_w[l], w_swa_proj[l], w_out[l],
                    ln2_w[l], w_ff1[l], w_ff2[l])
    return x2.reshape(batch, seq, d_model)
```

```python
import functools

import jax
import jax.numpy as jnp
from jax import lax
from jax.experimental import pallas as pl
from jax.experimental.pallas import tpu as pltpu

F32 = jnp.float32
BF16 = jnp.bfloat16

HEAD_DIM = 128
EPS = 1e-6
GDN_HEADS = 8
GDN_WIDTH = GDN_HEADS * HEAD_DIM
GDN_CONV = 4
GDN_CHUNK = 64
SWA_GROUPS = ((128, 1), (512, 4), (2048, 16))
SWA_HEADS_PER_GROUP = 4
SWA_GROUP_WIDTH = SWA_HEADS_PER_GROUP * HEAD_DIM
SWA_WIDTH = len(SWA_GROUPS) * SWA_GROUP_WIDTH
SWA_BLOCK = 128
ROPE_THETA = 10000.0
NEG_INF = -1e30

VMEM_LIMIT_BYTES = 52 * 1024 * 1024
LANES = 128
SUBLANES = 8
CONV_TAIL_ROWS = SUBLANES
NORM_ROWS = 256
GDN_BLOCK = 256
SWA_TOKENS = 2048
PROJ_TILE_N = 512


def _params(*sem):
    return pltpu.CompilerParams(dimension_semantics=sem, vmem_limit_bytes=VMEM_LIMIT_BYTES)


def _sigmoid(x):
    return jax.nn.sigmoid(x)


def _silu(x):
    return x * _sigmoid(x)


def _rope_table_kernel(pos_ref, invf_ref, cos_ref, sin_ref):
    ang = pos_ref[...].astype(F32) * invf_ref[...]
    cos_ref[...] = jnp.cos(ang)
    lane = lax.broadcasted_iota(jnp.int32, ang.shape, 1)
    s = jnp.sin(ang)
    sin_ref[...] = jnp.where(lane < HEAD_DIM // 2, -s, s)


def _rope_tables(pos_col, inv_freq2):
    t = pos_col.shape[0]
    tm = min(t, 2048)
    return pl.pallas_call(
        _rope_table_kernel,
        grid=(t // tm,),
        in_specs=[pl.BlockSpec((tm, 1), lambda i: (i, 0)),
                  pl.BlockSpec((1, HEAD_DIM), lambda i: (0, 0))],
        out_specs=[pl.BlockSpec((tm, HEAD_DIM), lambda i: (i, 0)),
                   pl.BlockSpec((tm, HEAD_DIM), lambda i: (i, 0))],
        out_shape=[jax.ShapeDtypeStruct((t, HEAD_DIM), F32)] * 2,
        compiler_params=_params("parallel"),
        name="rope_table",
    )(pos_col, inv_freq2)


def _rmsnorm_to(h_ref, x_ref, lnw_ref):
    tm = x_ref.shape[0]
    w = lnw_ref[...]

    def body(c, carry):
        rows = pl.ds(pl.multiple_of(c * NORM_ROWS, NORM_ROWS), NORM_ROWS)
        x = x_ref[rows, :]
        y = x * lax.rsqrt(jnp.mean(x * x, axis=-1, keepdims=True) + EPS)
        h_ref[rows, :] = (y * w).astype(BF16)
        return carry

    lax.fori_loop(0, tm // NORM_ROWS, body, 0)


def _in_proj_kernel(x_ref, lnw_ref, wab_ref, w_ref, ab_ref, *rest, tile_ranges):
    out_refs, h_ref = rest[:-1], rest[-1]
    j = pl.program_id(1)

    @pl.when(j == 0)
    def _():
        _rmsnorm_to(h_ref, x_ref, lnw_ref)
        ab_ref[...] = jnp.dot(h_ref[...], wab_ref[...], preferred_element_type=F32)

    acc = jnp.dot(h_ref[...], w_ref[...], preferred_element_type=F32).astype(BF16)
    for (lo, hi), o_ref in zip(tile_ranges, out_refs):
        @pl.when((j >= lo) & (j < hi))
        def _(o_ref=o_ref):
            o_ref[...] = acc


def _in_proj(x2, ln_w, w_cat, w_ab, widths, tm):
    t, k = x2.shape
    tn = PROJ_TILE_N
    ranges, lo = [], 0
    for wd in widths:
        ranges.append((lo, lo + wd // tn))
        lo += wd // tn
    n_tiles = lo

    def out_map(lo_, hi_):
        return lambda i, j: (i, jnp.clip(j - lo_, 0, hi_ - lo_ - 1))

    return pl.pallas_call(
        functools.partial(_in_proj_kernel, tile_ranges=tuple(ranges)),
        grid=(t // tm, n_tiles),
        in_specs=[pl.BlockSpec((tm, k), lambda i, j: (i, 0)),
                  pl.BlockSpec((1, k), lambda i, j: (0, 0)),
                  pl.BlockSpec((k, LANES), lambda i, j: (0, 0)),
                  pl.BlockSpec((k, tn), lambda i, j: (0, j))],
        out_specs=[pl.BlockSpec((tm, LANES), lambda i, j: (i, 0))]
        + [pl.BlockSpec((tm, tn), out_map(a, b)) for a, b in ranges],
        out_shape=[jax.ShapeDtypeStruct((t, LANES), F32)]
        + [jax.ShapeDtypeStruct((t, wd), BF16) for wd in widths],
        scratch_shapes=[pltpu.VMEM((tm, k), BF16)],
        compiler_params=_params("parallel", "arbitrary"),
        name="in_proj",
    )(x2, ln_w, w_ab, w_cat)


def _ff1_kernel(x_ref, lnw_ref, w_ref, o_ref, h_ref):
    @pl.when(pl.program_id(1) == 0)
    def _():
        _rmsnorm_to(h_ref, x_ref, lnw_ref)

    u = jnp.maximum(jnp.dot(h_ref[...], w_ref[...], preferred_element_type=F32), 0.0)
    o_ref[...] = (u * u).astype(o_ref.dtype)


def _ff1(x2, ln_w, w, tm, tn):
    t, k = x2.shape
    n = w.shape[1]
    return pl.pallas_call(
        _ff1_kernel,
        grid=(t // tm, n // tn),
        in_specs=[pl.BlockSpec((tm, k), lambda i, j: (i, 0)),
                  pl.BlockSpec((1, k), lambda i, j: (0, 0)),
                  pl.BlockSpec((k, tn), lambda i, j: (0, j))],
        out_specs=pl.BlockSpec((tm, tn), lambda i, j: (i, j)),
        out_shape=jax.ShapeDtypeStruct((t, n), BF16),
        scratch_shapes=[pltpu.VMEM((tm, k), BF16)],
        compiler_params=_params("parallel", "arbitrary"),
        name="ff1",
    )(x2, ln_w, w)


def _residual_matmul_kernel(a_ref, w_ref, r_ref, o_ref):
    o_ref[...] = r_ref[...] + jnp.dot(a_ref[...], w_ref[...], preferred_element_type=F32)


def _residual_matmul(a, w, resid, tm, tn, name):
    t, k = a.shape
    n = w.shape[1]
    return pl.pallas_call(
        _residual_matmul_kernel,
        grid=(t // tm, n // tn),
        in_specs=[pl.BlockSpec((tm, k), lambda i, j: (i, 0)),
                  pl.BlockSpec((k, tn), lambda i, j: (0, j)),
                  pl.BlockSpec((tm, tn), lambda i, j: (i, j))],
        out_specs=pl.BlockSpec((tm, tn), lambda i, j: (i, j)),
        out_shape=jax.ShapeDtypeStruct((t, n), F32),
        compiler_params=_params("parallel", "arbitrary"),
        name=name,
    )(a, w, resid)


def _gdn_kernel(qkvz_ref, ab_ref, convw_ref, alog_ref, dtb_ref, normw_ref, o_ref,
                xs_ref, act_ref, kt_ref, g_ref, gt_ref, beta_ref, state_ref):
    nbatch, tb = qkvz_ref.shape[0], qkvz_ref.shape[1]
    c_len, d, nh, w = GDN_CHUNK, HEAD_DIM, GDN_HEADS, GDN_WIDTH
    n_chunks = tb // c_len

    @pl.when(pl.program_id(0) == 0)
    def _():
        xs_ref[:, 0:CONV_TAIL_ROWS, :] = jnp.zeros((nbatch, CONV_TAIL_ROWS, 3 * w), F32)
        state_ref[...] = jnp.zeros(state_ref.shape, F32)

    ri = lax.broadcasted_iota(jnp.int32, (tb, tb), 0)
    ci = lax.broadcasted_iota(jnp.int32, (tb, tb), 1)
    tri = ((ci <= ri) & (ri // c_len == ci // c_len)).astype(F32)

    for b in range(nbatch):
        xs_ref[b, CONV_TAIL_ROWS:CONV_TAIL_ROWS + tb, :] = qkvz_ref[b, :, 0:3 * w].astype(F32)
        for cg in range(3 * nh):
            cols = slice(cg * d, (cg + 1) * d)
            acc = (xs_ref[b, CONV_TAIL_ROWS:CONV_TAIL_ROWS + tb, cols]
                   * convw_ref[GDN_CONV - 1:GDN_CONV, cols])
            for s in range(1, GDN_CONV):
                acc = acc + (xs_ref[b, CONV_TAIL_ROWS - s:CONV_TAIL_ROWS - s + tb, cols]
                             * convw_ref[GDN_CONV - 1 - s:GDN_CONV - s, cols])
            y = _silu(acc)
            if cg < 2 * nh:
                y = y * lax.rsqrt(jnp.sum(y * y, axis=-1, keepdims=True) + EPS)
                if cg < nh:
                    y = y * (d ** -0.5)
            act_ref[b, :, cols] = y
            if nh <= cg < 2 * nh:
                for c in range(n_chunks):
                    kt_ref[b, c, cg - nh] = y[c * c_len:(c + 1) * c_len, :].T
        xs_ref[b, 0:CONV_TAIL_ROWS, :] = xs_ref[b, tb:tb + CONV_TAIL_ROWS, :]

        ab = ab_ref[b]
        beta_ref[b] = _sigmoid(ab)
        z = ab + dtb_ref[...]
        softplus = jnp.maximum(z, 0.0) + jnp.log1p(jnp.exp(-jnp.abs(z)))
        g = -jnp.exp(alog_ref[...]) * softplus
        g_cum = jnp.dot(tri, g, precision=lax.Precision.HIGHEST, preferred_element_type=F32)
        g_ref[b] = g_cum
        for c in range(n_chunks):
            gt_ref[b, c] = g_cum[c * c_len:(c + 1) * c_len, :].T

    ii = lax.broadcasted_iota(jnp.int32, (c_len, c_len), 0)
    jj = lax.broadcasted_iota(jnp.int32, (c_len, c_len), 1)
    incl = ii >= jj
    strict = ii > jj
    norm_w = normw_ref[...]
    chains = [(b, h) for b in range(nbatch) for h in range(nh)]

    def mm(a, b):
        return jnp.dot(a.astype(BF16), b.astype(BF16), preferred_element_type=F32)

    def chunk_body(c, carry):
        rows = pl.ds(pl.multiple_of(c * c_len, c_len), c_len)
        g_all = [g_ref[b, rows, :] for b in range(nbatch)]
        beta_all = [beta_ref[b, rows, :] for b in range(nbatch)]
        gt_all = [gt_ref[b, c] for b in range(nbatch)]

        st = []
        for b, h in chains:
            q = act_ref[b, rows, h * d:(h + 1) * d]
            k = act_ref[b, rows, w + h * d:w + (h + 1) * d]
            v = act_ref[b, rows, 2 * w + h * d:2 * w + (h + 1) * d]
            kt = kt_ref[b, c, h]
            g_col = g_all[b][:, h:h + 1]
            b_col = beta_all[b][:, nh + h:nh + h + 1]
            g_row = gt_all[b][h:h + 1, :]
            g_last = g_row[:, c_len - 1:c_len]
            decay = jnp.where(incl, jnp.exp(jnp.where(incl, g_col - g_row, 0.0)), 0.0)
            e_g = jnp.exp(g_col)
            kb = k * b_col
            st.append(dict(q=q, kt=kt, g_row=g_row, g_last=g_last, decay=decay, kb=kb,
                           qe=q * e_g, rhs=jnp.concatenate([v * b_col, kb * e_g], axis=1)))
        for s in st:
            s1 = mm(jnp.concatenate([s["kb"], s["q"]], axis=0), s["kt"])
            s["a"] = jnp.where(strict, s1[:c_len] * s["decay"], 0.0)
            s["a_qk"] = jnp.where(incl, s1[c_len:] * s["decay"], 0.0)
        for s in st:
            s["x"] = s["rhs"] - mm(s["a"], s["rhs"])
            s["m"] = mm(s["a"], s["a"])
        for it in range(5):
            for s in st:
                s["x"] = s["x"] + mm(s["m"], s["x"])
                if it < 4:
                    s["m"] = mm(s["m"], s["m"])
        for (b, h), s in zip(chains, st):
            s["state"] = state_ref[b, h]
            s["ws"] = mm(jnp.concatenate([s["x"][:, d:], s["qe"]], axis=0), s["state"])
        for (b, h), s in zip(chains, st):
            v_new = s["x"][:, :d] - s["ws"][:c_len]
            o = s["ws"][c_len:] + mm(s["a_qk"], v_new)
            kdt = s["kt"] * jnp.exp(s["g_last"] - s["g_row"])
            state_ref[b, h] = s["state"] * jnp.exp(s["g_last"]) + mm(kdt, v_new)
            o = o * lax.rsqrt(jnp.mean(o * o, axis=-1, keepdims=True) + EPS)
            zh = qkvz_ref[b, rows, 3 * w + h * d:3 * w + (h + 1) * d].astype(F32)
            o_ref[b, rows, h * d:(h + 1) * d] = (o * norm_w * _silu(zh)).astype(o_ref.dtype)
        return carry

    lax.fori_loop(0, n_chunks, chunk_body, 0)


def _gdn(main, ab, conv_w, a_log, dt_bias, norm_w, batch, seq):
    tb = GDN_BLOCK
    w = GDN_WIDTH
    n_chunks = tb // GDN_CHUNK
    blk = lambda s: (0, s, 0)
    const = lambda s: (0, 0)
    out = pl.pallas_call(
        _gdn_kernel,
        grid=(seq // tb,),
        in_specs=[pl.BlockSpec((batch, tb, 4 * w), blk),
                  pl.BlockSpec((batch, tb, LANES), blk),
                  pl.BlockSpec((GDN_CONV, 3 * w), const),
                  pl.BlockSpec((1, LANES), const),
                  pl.BlockSpec((1, LANES), const),
                  pl.BlockSpec((1, HEAD_DIM), const)],
        out_specs=pl.BlockSpec((batch, tb, w), blk),
        out_shape=jax.ShapeDtypeStruct((batch, seq, w), BF16),
        scratch_shapes=[pltpu.VMEM((batch, tb + CONV_TAIL_ROWS, 3 * w), F32),
                        pltpu.VMEM((batch, tb, 3 * w), F32),
                        pltpu.VMEM((batch, n_chunks, GDN_HEADS, HEAD_DIM, GDN_CHUNK), F32),
                        pltpu.VMEM((batch, tb, LANES), F32),
                        pltpu.VMEM((batch, n_chunks, LANES, GDN_CHUNK), F32),
                        pltpu.VMEM((batch, tb, LANES), F32),
                        pltpu.VMEM((batch, GDN_HEADS, HEAD_DIM, HEAD_DIM), F32)],
        compiler_params=_params("arbitrary"),
        name="gdn",
    )(main.reshape(batch, seq, main.shape[1]), ab.reshape(batch, seq, LANES), conv_w, a_log, dt_bias, norm_w)
    return out.reshape(batch * seq, w)


def _swa_kernel(x_ref, cos_ref, sin_ref, qnw_ref, knw_ref, o_ref,
                q_s, k_s0, k_s1, k_s2, v_s0, v_s1, v_s2, og_s, lg_s):
    tq, d, blk = x_ref.shape[0], HEAD_DIM, SWA_BLOCK
    first = pl.program_id(2) == 0
    k_ss, v_ss = (k_s0, k_s1, k_s2), (v_s0, v_s1, v_s2)
    cos, sin = cos_ref[...], sin_ref[...]
    qi = lax.broadcasted_iota(jnp.int32, (blk, 2 * blk), 0)
    cj = lax.broadcasted_iota(jnp.int32, (blk, 2 * blk), 1)
    band = (cj >= qi) & (cj <= qi + blk)
    band_first = band & (cj >= jnp.where(first, blk, 0))
    ones_sq = jnp.ones((2 * d, 2 * d), BF16)
    ones_v = jnp.ones((2 * blk, d), BF16)

    def norm_rope(x, nw):
        x = x.astype(F32)
        sq = x * x
        hi = sq.astype(BF16)
        lo = (sq - hi.astype(F32)).astype(BF16)
        ssq = jnp.dot(jnp.concatenate([hi, lo], axis=1), ones_sq, preferred_element_type=F32)[:, :d]
        x = x * lax.rsqrt(ssq * (1.0 / d) + EPS) * nw
        return x * cos + pltpu.roll(x, d // 2, 1) * sin

    for g, (window, dl) in enumerate(SWA_GROUPS):
        tail = blk * dl
        k_s, v_s = k_ss[g], v_ss[g]

        @pl.when(first)
        def _(k_s=k_s, v_s=v_s, tail=tail):
            k_s[0:tail, :] = jnp.zeros((tail, d), F32)
            v_s[0:tail, :] = jnp.zeros((tail, d), F32)

        base = 3 * g * d
        q_s[...] = norm_rope(x_ref[:, base:base + d], qnw_ref[...]) * (d ** -0.5)
        k_s[tail:tail + tq, :] = norm_rope(x_ref[:, base + d:base + 2 * d], knw_ref[...])
        v_s[tail:tail + tq, :] = x_ref[:, base + 2 * d:base + 3 * d].astype(F32)
        for r in range(dl):
            for i in range(tq // tail):
                start = tail * i + r
                if dl == 1:
                    qrows, krows = pl.ds(start, blk), pl.ds(start, 2 * blk)
                else:
                    qrows, krows = pl.ds(start, blk, stride=dl), pl.ds(start, 2 * blk, stride=dl)
                q = q_s[qrows, :].astype(BF16)
                k = k_s[krows, :].astype(BF16)
                v = jnp.concatenate([v_s[krows, :].astype(BF16), ones_v], axis=1)
                s = lax.dot_general(q, k, (((1,), (1,)), ((), ())), preferred_element_type=F32)
                s = jnp.where(band_first if i == 0 else band, s, NEG_INF)
                mx = jnp.max(s, axis=-1, keepdims=True)
                p = jnp.exp(s - mx)
                pv = jnp.dot(p.astype(BF16), v, preferred_element_type=F32)
                den = pv[:, d:]
                og_s[g, qrows, :] = pv[:, :d] / den
                lg_s[g, qrows, :] = mx + jnp.log(den)
        k_s[0:tail, :] = k_s[tq:tq + tail, :]
        v_s[0:tail, :] = v_s[tq:tq + tail, :]

    l0, l1, l2 = lg_s[0], lg_s[1], lg_s[2]
    mx = jnp.maximum(jnp.maximum(l0, l1), l2)
    e0, e1, e2 = jnp.exp(l0 - mx), jnp.exp(l1 - mx), jnp.exp(l2 - mx)
    o_ref[...] = ((og_s[0] * e0 + og_s[1] * e1 + og_s[2] * e2) / (e0 + e1 + e2)).astype(o_ref.dtype)


def _swa(xs, cos_t, sin_t, qnw, knw, batch, seq):
    t = xs.shape[0]
    d = HEAD_DIM
    tq = SWA_TOKENS
    nb = seq // tq
    row = lambda b, j, n: (b * nb + n, 0)
    slot = lambda b, j, n: (b * nb + n, j)
    const = lambda b, j, n: (0, 0)
    dils = [dl for _, dl in SWA_GROUPS]
    return pl.pallas_call(
        _swa_kernel,
        grid=(batch, SWA_HEADS_PER_GROUP, nb),
        in_specs=[pl.BlockSpec((tq, 3 * len(SWA_GROUPS) * d), slot),
                  pl.BlockSpec((tq, d), row), pl.BlockSpec((tq, d), row),
                  pl.BlockSpec((1, d), const), pl.BlockSpec((1, d), const)],
        out_specs=pl.BlockSpec((tq, d), slot),
        out_shape=jax.ShapeDtypeStruct((t, SWA_GROUP_WIDTH), BF16),
        scratch_shapes=[pltpu.VMEM((tq, d), F32)]
        + [pltpu.VMEM((SWA_BLOCK * dl + tq, d), F32) for dl in dils] * 2
        + [pltpu.VMEM((len(SWA_GROUPS), tq, d), F32)] * 2,
        compiler_params=_params("parallel", "parallel", "arbitrary"),
        name="swa",
    )(xs, cos_t, sin_t, qnw, knw)


def _mix_kernel(oa_ref, ob_ref, wg_ref, ws_ref, ga_ref, gb_ref, out_ref):
    ya = jnp.dot(oa_ref[...], wg_ref[...], preferred_element_type=F32)
    yb = jnp.dot(ob_ref[...], ws_ref[...], preferred_element_type=F32)
    mixed = _sigmoid(ga_ref[...].astype(F32)) * ya + _sigmoid(gb_ref[...].astype(F32)) * yb
    out_ref[...] = mixed.astype(out_ref.dtype)


def _mix(oa, ob, wg, ws, main, gate_block, tm):
    t = oa.shape[0]
    n = wg.shape[1]
    row = lambda i: (i, 0)
    const = lambda i: (0, 0)
    return pl.pallas_call(
        _mix_kernel,
        grid=(t // tm,),
        in_specs=[pl.BlockSpec((tm, oa.shape[1]), row), pl.BlockSpec((tm, ob.shape[1]), row),
                  pl.BlockSpec(wg.shape, const), pl.BlockSpec(ws.shape, const),
                  pl.BlockSpec((tm, n), lambda i: (i, gate_block)),
                  pl.BlockSpec((tm, n), lambda i: (i, gate_block + 1))],
        out_specs=pl.BlockSpec((tm, n), row),
        out_shape=jax.ShapeDtypeStruct((t, n), BF16),
        compiler_params=_params("parallel"),
        name="mix",
    )(oa, ob, wg, ws, main, main)


def _layer(x2, positions, batch, seq, ln1_w, w_in, conv_w, a_log, dt_bias, gdn_norm_w, w_gdn_proj,
           q_norm_w, k_norm_w, w_swa_proj, w_out, ln2_w, w_ff1, w_ff2):
    d_model = x2.shape[1]
    gw3 = 3 * GDN_WIDTH
    o_a, o_b = gw3 + GDN_WIDTH, gw3 + GDN_WIDTH + GDN_HEADS
    o_swa = o_b + GDN_HEADS
    o_ga = o_swa + 3 * SWA_WIDTH
    assert seq % SWA_TOKENS == 0 and seq % GDN_BLOCK == 0 and d_model % PROJ_TILE_N == 0
    assert all(window // dl == SWA_BLOCK and SWA_TOKENS % (dl * SWA_BLOCK) == 0 for window, dl in SWA_GROUPS)
    swa_cols = []
    for j in range(SWA_HEADS_PER_GROUP):
        for g in range(len(SWA_GROUPS)):
            for part in range(3):
                lo = o_swa + part * SWA_WIDTH + (g * SWA_HEADS_PER_GROUP + j) * HEAD_DIM
                swa_cols.append(w_in[:, lo:lo + HEAD_DIM])
    w_cat = jnp.concatenate([w_in[:, :o_a], w_in[:, o_ga:]] + swa_cols, axis=1).astype(BF16)
    w_ab = jnp.pad(w_in[:, o_a:o_swa], ((0, 0), (0, LANES - 2 * GDN_HEADS))).astype(BF16)
    widths = (4 * GDN_WIDTH + 2 * d_model, 3 * SWA_WIDTH)

    ab, main, swa_in = _in_proj(x2, ln1_w.reshape(1, -1), w_cat, w_ab, widths, tm=1024)

    pad8 = lambda v: jnp.pad(v.astype(F32), (0, LANES - v.shape[0])).reshape(1, LANES)
    o_a_branch = _gdn(main, ab, conv_w.astype(F32), pad8(a_log), pad8(dt_bias),
                      gdn_norm_w.reshape(1, -1).astype(F32), batch, seq)

    half = HEAD_DIM // 2
    inv_freq = ROPE_THETA ** (-jnp.arange(half, dtype=F32) / half)
    cos_t, sin_t = _rope_tables(positions.reshape(-1, 1), jnp.concatenate([inv_freq, inv_freq]).reshape(1, HEAD_DIM))
    o_b_branch = _swa(swa_in, cos_t, sin_t, q_norm_w.reshape(1, -1).astype(F32),
                      k_norm_w.reshape(1, -1).astype(F32), batch, seq)

    mixed = _mix(o_a_branch, o_b_branch, w_gdn_proj.astype(BF16), w_swa_proj.astype(BF16), main,
                 gate_block=4 * GDN_WIDTH // d_model, tm=512)
    x2 = _residual_matmul(mixed, w_out.astype(BF16), x2, tm=512, tn=d_model, name="out_proj")
    u2 = _ff1(x2, ln2_w.reshape(1, -1), w_ff1.astype(BF16), tm=1024, tn=PROJ_TILE_N)
    return _residual_matmul(u2, w_ff2.astype(BF16), x2, tm=512, tn=PROJ_TILE_N, name="ff2")


def kernel(x, positions, ln1_w, w_in, gdn_conv_w, gdn_a_log, gdn_dt_bias, gdn_norm_w, w_gdn_proj,
           swa_q_norm_w, swa_k_norm_w, w_swa_proj, w_out, ln2_w, w_ff1, w_ff2):
    batch, seq, d_model = x.shape
    x2 = x.reshape(batch * seq, d_model)
    for l in range(ln1_w.shape[0]):
        x2 = _layer(x2, positions, batch, seq, ln1_w[l], w_in[l], gdn_conv_w[l], gdn_a_log[l], gdn_dt_bias[l],
                    gdn_norm_w[l], w_gdn_proj[l], swa_q_norm_w[l], swa_k_norm_w[l], w_swa_proj[l], w_out[l],
                    ln2_w[l], w_ff1[l], w_ff2[l])
    return x2.reshape(batch, seq, d_model)
```

```python
import functools

import jax
import jax.numpy as jnp
from jax import lax
from jax.experimental import pallas as pl
from jax.experimental.pallas import tpu as pltpu

F32 = jnp.float32
BF16 = jnp.bfloat16

HEAD_DIM = 128
EPS = 1e-6
GDN_HEADS = 8
GDN_WIDTH = GDN_HEADS * HEAD_DIM
GDN_CONV = 4
GDN_CHUNK = 64
SWA_GROUPS = ((128, 1), (512, 4), (2048, 16))
SWA_HEADS_PER_GROUP = 4
SWA_GROUP_WIDTH = SWA_HEADS_PER_GROUP * HEAD_DIM
SWA_WIDTH = len(SWA_GROUPS) * SWA_GROUP_WIDTH
SWA_BLOCK = 128
ROPE_THETA = 10000.0
NEG_INF = -1e30

VMEM_LIMIT_BYTES = 52 * 1024 * 1024
LANES = 128
SUBLANES = 8
CONV_TAIL_ROWS = SUBLANES
NORM_ROWS = 256
GDN_BLOCK = 256
SWA_TOKENS = 2048


def _params(*sem):
    return pltpu.CompilerParams(dimension_semantics=sem, vmem_limit_bytes=VMEM_LIMIT_BYTES)


def _sigmoid(x):
    return 0.5 * jnp.tanh(0.5 * x) + 0.5


def _silu(x):
    h = 0.5 * x
    return h * jnp.tanh(h) + h


def _rope_table_kernel(pos_ref, invf_ref, cos_ref, sin_ref):
    ang = pos_ref[...].astype(F32) * invf_ref[...]
    cos_ref[...] = jnp.cos(ang)
    lane = lax.broadcasted_iota(jnp.int32, ang.shape, 1)
    s = jnp.sin(ang)
    sin_ref[...] = jnp.where(lane < HEAD_DIM // 2, -s, s)


def _rope_tables(pos_col, inv_freq2):
    t = pos_col.shape[0]
    tm = min(t, 2048)
    return pl.pallas_call(
        _rope_table_kernel,
        grid=(t // tm,),
        in_specs=[pl.BlockSpec((tm, 1), lambda i: (i, 0)),
                  pl.BlockSpec((1, HEAD_DIM), lambda i: (0, 0))],
        out_specs=[pl.BlockSpec((tm, HEAD_DIM), lambda i: (i, 0)),
                   pl.BlockSpec((tm, HEAD_DIM), lambda i: (i, 0))],
        out_shape=[jax.ShapeDtypeStruct((t, HEAD_DIM), F32)] * 2,
        compiler_params=_params("parallel"),
        name="rope_table",
    )(pos_col, inv_freq2)


def _rmsnorm_to(h_ref, x_ref, lnw_ref):
    tm = x_ref.shape[0]
    w = lnw_ref[...]

    def body(c, carry):
        rows = pl.ds(pl.multiple_of(c * NORM_ROWS, NORM_ROWS), NORM_ROWS)
        x = x_ref[rows, :]
        y = x * lax.rsqrt(jnp.mean(x * x, axis=-1, keepdims=True) + EPS)
        h_ref[rows, :] = (y * w).astype(BF16)
        return carry

    lax.fori_loop(0, tm // NORM_ROWS, body, 0)


def _norm_matmul_kernel(x_ref, lnw_ref, w_ref, *rest, with_ab, square_relu):
    if with_ab:
        wab_ref, o_ref, ab_ref, h_ref = rest
    else:
        o_ref, h_ref = rest

    @pl.when(pl.program_id(1) == 0)
    def _():
        _rmsnorm_to(h_ref, x_ref, lnw_ref)
        if with_ab:
            ab_ref[...] = jnp.dot(h_ref[...], wab_ref[...], preferred_element_type=F32)

    acc = jnp.dot(h_ref[...], w_ref[...], preferred_element_type=F32)
    if square_relu:
        acc = jnp.maximum(acc, 0.0)
        acc = acc * acc
    o_ref[...] = acc.astype(o_ref.dtype)


def _norm_matmul(x2, ln_w, w, tm, tn, name, w_ab=None, square_relu=False):
    t, k = x2.shape
    n = w.shape[1]
    in_specs = [pl.BlockSpec((tm, k), lambda i, j: (i, 0)),
                pl.BlockSpec((1, k), lambda i, j: (0, 0)),
                pl.BlockSpec((k, tn), lambda i, j: (0, j))]
    out_specs = [pl.BlockSpec((tm, tn), lambda i, j: (i, j))]
    out_shape = [jax.ShapeDtypeStruct((t, n), BF16)]
    args = [x2, ln_w, w]
    if w_ab is not None:
        in_specs.append(pl.BlockSpec((k, LANES), lambda i, j: (0, 0)))
        out_specs.append(pl.BlockSpec((tm, LANES), lambda i, j: (i, 0)))
        out_shape.append(jax.ShapeDtypeStruct((t, LANES), F32))
        args.append(w_ab)
    out = pl.pallas_call(
        functools.partial(_norm_matmul_kernel, with_ab=w_ab is not None, square_relu=square_relu),
        grid=(t // tm, n // tn),
        in_specs=in_specs, out_specs=out_specs, out_shape=out_shape,
        scratch_shapes=[pltpu.VMEM((tm, k), BF16)],
        compiler_params=_params("parallel", "arbitrary"),
        name=name,
    )(*args)
    return out if w_ab is not None else out[0]


def _residual_matmul_kernel(a_ref, w_ref, r_ref, o_ref):
    @pl.when(pl.program_id(1) == 0)
    def _():
        o_ref[...] = r_ref[...]

    o_ref[...] += jnp.dot(a_ref[...], w_ref[...], preferred_element_type=F32)


def _residual_matmul(a, w, resid, tm, tk, name):
    t, k = a.shape
    n = w.shape[1]
    return pl.pallas_call(
        _residual_matmul_kernel,
        grid=(t // tm, k // tk),
        in_specs=[pl.BlockSpec((tm, tk), lambda i, kk: (i, kk)),
                  pl.BlockSpec((tk, n), lambda i, kk: (kk, 0)),
                  pl.BlockSpec((tm, n), lambda i, kk: (i, 0))],
        out_specs=pl.BlockSpec((tm, n), lambda i, kk: (i, 0)),
        out_shape=jax.ShapeDtypeStruct((t, n), F32),
        compiler_params=_params("parallel", "arbitrary"),
        name=name,
    )(a, w, resid)


def _gdn_kernel(qkvz_ref, ab_ref, convw_ref, alog_ref, dtb_ref, normw_ref, o_ref,
                xs_ref, act_ref, kt_ref, g_ref, gt_ref, beta_ref, state_ref):
    nbatch, tb = qkvz_ref.shape[0], qkvz_ref.shape[1]
    c_len, d, nh, w = GDN_CHUNK, HEAD_DIM, GDN_HEADS, GDN_WIDTH
    n_chunks = tb // c_len

    @pl.when(pl.program_id(0) == 0)
    def _():
        xs_ref[:, 0:CONV_TAIL_ROWS, :] = jnp.zeros((nbatch, CONV_TAIL_ROWS, 3 * w), F32)
        state_ref[...] = jnp.zeros(state_ref.shape, F32)

    ri = lax.broadcasted_iota(jnp.int32, (tb, tb), 0)
    ci = lax.broadcasted_iota(jnp.int32, (tb, tb), 1)
    tri = ((ci <= ri) & (ri // c_len == ci // c_len)).astype(F32)

    for b in range(nbatch):
        xs_ref[b, CONV_TAIL_ROWS:CONV_TAIL_ROWS + tb, :] = qkvz_ref[b, :, 0:3 * w].astype(F32)
        for cg in range(3 * nh):
            cols = slice(cg * d, (cg + 1) * d)
            acc = (xs_ref[b, CONV_TAIL_ROWS:CONV_TAIL_ROWS + tb, cols]
                   * convw_ref[GDN_CONV - 1:GDN_CONV, cols])
            for s in range(1, GDN_CONV):
                acc = acc + (xs_ref[b, CONV_TAIL_ROWS - s:CONV_TAIL_ROWS - s + tb, cols]
                             * convw_ref[GDN_CONV - 1 - s:GDN_CONV - s, cols])
            y = _silu(acc)
            if cg < 2 * nh:
                y = y * lax.rsqrt(jnp.sum(y * y, axis=-1, keepdims=True) + EPS)
                if cg < nh:
                    y = y * (d ** -0.5)
            act_ref[b, :, cols] = y
            if nh <= cg < 2 * nh:
                for c in range(n_chunks):
                    kt_ref[b, c, cg - nh] = y[c * c_len:(c + 1) * c_len, :].T
        xs_ref[b, 0:CONV_TAIL_ROWS, :] = xs_ref[b, tb:tb + CONV_TAIL_ROWS, :]

        ab = ab_ref[b]
        beta_ref[b] = _sigmoid(ab)
        z = ab + dtb_ref[...]
        softplus = jnp.maximum(z, 0.0) + jnp.log1p(jnp.exp(-jnp.abs(z)))
        g = -jnp.exp(alog_ref[...]) * softplus
        g_cum = jnp.dot(tri, g, precision=lax.Precision.HIGHEST, preferred_element_type=F32)
        g_ref[b] = g_cum
        for c in range(n_chunks):
            gt_ref[b, c] = g_cum[c * c_len:(c + 1) * c_len, :].T

    ii = lax.broadcasted_iota(jnp.int32, (c_len, c_len), 0)
    jj = lax.broadcasted_iota(jnp.int32, (c_len, c_len), 1)
    incl = ii >= jj
    strict = ii > jj
    norm_w = normw_ref[...]
    chains = [(b, h) for b in range(nbatch) for h in range(nh)]

    def mm(a, b):
        return jnp.dot(a.astype(BF16), b.astype(BF16), preferred_element_type=F32)

    def chunk_body(c, carry):
        rows = slice(c * c_len, (c + 1) * c_len)
        g_all = [g_ref[b, rows, :] for b in range(nbatch)]
        beta_all = [beta_ref[b, rows, :] for b in range(nbatch)]
        gt_all = [gt_ref[b, c] for b in range(nbatch)]

        st = []
        for b, h in chains:
            q = act_ref[b, rows, h * d:(h + 1) * d]
            k = act_ref[b, rows, w + h * d:w + (h + 1) * d]
            v = act_ref[b, rows, 2 * w + h * d:2 * w + (h + 1) * d]
            kt = kt_ref[b, c, h]
            g_col = g_all[b][:, h:h + 1]
            b_col = beta_all[b][:, nh + h:nh + h + 1]
            g_row = gt_all[b][h:h + 1, :]
            g_last = g_row[:, c_len - 1:c_len]
            decay = jnp.where(incl, jnp.exp(jnp.where(incl, g_col - g_row, 0.0)), 0.0)
            e_g = jnp.exp(g_col)
            kb = k * b_col
            st.append(dict(q=q, kt=kt, g_row=g_row, g_last=g_last, decay=decay, kb=kb,
                           qe=q * e_g, rhs=jnp.concatenate([v * b_col, kb * e_g], axis=1)))
        for s in st:
            s1 = mm(jnp.concatenate([s["kb"], s["q"]], axis=0), s["kt"])
            s["a"] = jnp.where(strict, s1[:c_len] * s["decay"], 0.0)
            s["a_qk"] = jnp.where(incl, s1[c_len:] * s["decay"], 0.0)
        for s in st:
            s["x"] = s["rhs"] - mm(s["a"], s["rhs"])
            s["m"] = mm(s["a"], s["a"])
        for it in range(5):
            for s in st:
                s["x"] = s["x"] + mm(s["m"], s["x"])
                if it < 4:
                    s["m"] = mm(s["m"], s["m"])
        for (b, h), s in zip(chains, st):
            s["state"] = state_ref[b, h]
            s["ws"] = mm(jnp.concatenate([s["x"][:, d:], s["qe"]], axis=0), s["state"])
        for (b, h), s in zip(chains, st):
            v_new = s["x"][:, :d] - s["ws"][:c_len]
            o = s["ws"][c_len:] + mm(s["a_qk"], v_new)
            kdt = s["kt"] * jnp.exp(s["g_last"] - s["g_row"])
            state_ref[b, h] = s["state"] * jnp.exp(s["g_last"]) + mm(kdt, v_new)
            o = o * lax.rsqrt(jnp.mean(o * o, axis=-1, keepdims=True) + EPS)
            zh = qkvz_ref[b, rows, 3 * w + h * d:3 * w + (h + 1) * d].astype(F32)
            o_ref[b, rows, h * d:(h + 1) * d] = (o * norm_w * _silu(zh)).astype(o_ref.dtype)
        return carry

    for c in range(n_chunks):
        chunk_body(c, 0)


def _gdn(main, ab, conv_w, a_log, dt_bias, norm_w, batch, seq):
    tb = GDN_BLOCK
    w = GDN_WIDTH
    n_chunks = tb // GDN_CHUNK
    blk = lambda s: (0, s, 0)
    const = lambda s: (0, 0)
    out = pl.pallas_call(
        _gdn_kernel,
        grid=(seq // tb,),
        in_specs=[pl.BlockSpec((batch, tb, 4 * w), blk),
                  pl.BlockSpec((batch, tb, LANES), blk),
                  pl.BlockSpec((GDN_CONV, 3 * w), const),
                  pl.BlockSpec((1, LANES), const),
                  pl.BlockSpec((1, LANES), const),
                  pl.BlockSpec((1, HEAD_DIM), const)],
        out_specs=pl.BlockSpec((batch, tb, w), blk),
        out_shape=jax.ShapeDtypeStruct((batch, seq, w), BF16),
        scratch_shapes=[pltpu.VMEM((batch, tb + CONV_TAIL_ROWS, 3 * w), F32),
                        pltpu.VMEM((batch, tb, 3 * w), F32),
                        pltpu.VMEM((batch, n_chunks, GDN_HEADS, HEAD_DIM, GDN_CHUNK), F32),
                        pltpu.VMEM((batch, tb, LANES), F32),
                        pltpu.VMEM((batch, n_chunks, LANES, GDN_CHUNK), F32),
                        pltpu.VMEM((batch, tb, LANES), F32),
                        pltpu.VMEM((batch, GDN_HEADS, HEAD_DIM, HEAD_DIM), F32)],
        compiler_params=_params("arbitrary"),
        name="gdn",
    )(main.reshape(batch, seq, main.shape[1]), ab.reshape(batch, seq, LANES), conv_w, a_log, dt_bias, norm_w)
    return out.reshape(batch * seq, w)


def _swa_kernel(x_ref, cos_ref, sin_ref, qnw_ref, knw_ref, o_ref,
                q_s, k_s0, k_s1, k_s2, v_s0, v_s1, v_s2, og_s, lg_s):
    tq, d, blk = x_ref.shape[0], HEAD_DIM, SWA_BLOCK
    first = pl.program_id(2) == 0
    k_ss, v_ss = (k_s0, k_s1, k_s2), (v_s0, v_s1, v_s2)
    qi = lax.broadcasted_iota(jnp.int32, (blk, 2 * blk), 0)
    cj = lax.broadcasted_iota(jnp.int32, (blk, 2 * blk), 1)
    band = (cj >= qi) & (cj <= qi + blk)
    band_first = band & (cj >= jnp.where(first, blk, 0))
    mean_mat = jnp.full((2 * d, 2 * d), 1.0 / d, BF16)
    ones_v = jnp.ones((2 * blk, d), BF16)
    qnw = qnw_ref[...] * (d ** -0.5)
    knw = knw_ref[...]

    def norm_rope(x, nw, cos, sin):
        x = x.astype(F32)
        sq = x * x
        hi = sq.astype(BF16)
        lo = (sq - hi.astype(F32)).astype(BF16)
        msq = jnp.dot(jnp.concatenate([hi, lo], axis=1), mean_mat, preferred_element_type=F32)[:, :d]
        x = x * lax.rsqrt(msq + EPS) * nw
        return x * cos + pltpu.roll(x, d // 2, 1) * sin

    for g, (window, dl) in enumerate(SWA_GROUPS):
        tail = blk * dl
        k_s, v_s = k_ss[g], v_ss[g]

        @pl.when(first)
        def _(k_s=k_s, v_s=v_s, tail=tail):
            k_s[0:tail, :] = jnp.zeros((tail, d), F32)
            v_s[0:tail, :] = jnp.zeros((tail, d), F32)

        base = 3 * g * d
        for c0 in range(0, tq, NORM_ROWS):
            rr = slice(c0, c0 + NORM_ROWS)
            kr = slice(tail + c0, tail + c0 + NORM_ROWS)
            cos, sin = cos_ref[rr, :], sin_ref[rr, :]
            q_s[rr, :] = norm_rope(x_ref[rr, base:base + d], qnw, cos, sin)
            k_s[kr, :] = norm_rope(x_ref[rr, base + d:base + 2 * d], knw, cos, sin)
            v_s[kr, :] = x_ref[rr, base + 2 * d:base + 3 * d].astype(F32)
        for r in range(dl):
            for i in range(tq // tail):
                start = tail * i + r
                if dl == 1:
                    qrows, krows = pl.ds(start, blk), pl.ds(start, 2 * blk)
                else:
                    qrows, krows = pl.ds(start, blk, stride=dl), pl.ds(start, 2 * blk, stride=dl)
                q = q_s[qrows, :].astype(BF16)
                k = k_s[krows, :].astype(BF16)
                v = jnp.concatenate([v_s[krows, :].astype(BF16), ones_v], axis=1)
                s = lax.dot_general(q, k, (((1,), (1,)), ((), ())), preferred_element_type=F32)
                s = jnp.where(band_first if i == 0 else band, s, NEG_INF)
                mx = jnp.max(s, axis=-1, keepdims=True)
                p = jnp.exp(s - mx)
                pv = jnp.dot(p.astype(BF16), v, preferred_element_type=F32)
                den = pv[:, d:]
                og_s[g, qrows, :] = pv[:, :d] / den
                lg_s[g, qrows, :] = mx + jnp.log(den)
        k_s[0:tail, :] = k_s[tq:tq + tail, :]
        v_s[0:tail, :] = v_s[tq:tq + tail, :]

    l0, l1, l2 = lg_s[0], lg_s[1], lg_s[2]
    mx = jnp.maximum(jnp.maximum(l0, l1), l2)
    e0, e1, e2 = jnp.exp(l0 - mx), jnp.exp(l1 - mx), jnp.exp(l2 - mx)
    o_ref[...] = ((og_s[0] * e0 + og_s[1] * e1 + og_s[2] * e2) / (e0 + e1 + e2)).astype(o_ref.dtype)


def _swa(xs, cos_t, sin_t, qnw, knw, batch, seq):
    t = xs.shape[0]
    d = HEAD_DIM
    tq = SWA_TOKENS
    nb = seq // tq
    row = lambda b, j, n: (b * nb + n, 0)
    slot = lambda b, j, n: (b * nb + n, j)
    const = lambda b, j, n: (0, 0)
    dils = [dl for _, dl in SWA_GROUPS]
    return pl.pallas_call(
        _swa_kernel,
        grid=(batch, SWA_HEADS_PER_GROUP, nb),
        in_specs=[pl.BlockSpec((tq, 3 * len(SWA_GROUPS) * d), slot),
                  pl.BlockSpec((tq, d), row), pl.BlockSpec((tq, d), row),
                  pl.BlockSpec((1, d), const), pl.BlockSpec((1, d), const)],
        out_specs=pl.BlockSpec((tq, d), slot),
        out_shape=jax.ShapeDtypeStruct((t, SWA_GROUP_WIDTH), BF16),
        scratch_shapes=[pltpu.VMEM((tq, d), F32)]
        + [pltpu.VMEM((SWA_BLOCK * dl + tq, d), F32) for dl in dils] * 2
        + [pltpu.VMEM((len(SWA_GROUPS), tq, d), F32)] * 2,
        compiler_params=_params("parallel", "parallel", "arbitrary"),
        name="swa",
    )(xs, cos_t, sin_t, qnw, knw)


def _mix_kernel(oa_ref, ob_ref, wg_ref, ws_ref, ga_ref, gb_ref, out_ref):
    ya = jnp.dot(oa_ref[...], wg_ref[...], preferred_element_type=F32)
    yb = jnp.dot(ob_ref[...], ws_ref[...], preferred_element_type=F32)
    mixed = _sigmoid(ga_ref[...].astype(F32)) * ya + _sigmoid(gb_ref[...].astype(F32)) * yb
    out_ref[...] = mixed.astype(out_ref.dtype)


def _mix(oa, ob, wg, ws, main, gate_block, tm):
    t = oa.shape[0]
    n = wg.shape[1]
    row = lambda i: (i, 0)
    const = lambda i: (0, 0)
    return pl.pallas_call(
        _mix_kernel,
        grid=(t // tm,),
        in_specs=[pl.BlockSpec((tm, oa.shape[1]), row), pl.BlockSpec((tm, ob.shape[1]), row),
                  pl.BlockSpec(wg.shape, const), pl.BlockSpec(ws.shape, const),
                  pl.BlockSpec((tm, n), lambda i: (i, gate_block)),
                  pl.BlockSpec((tm, n), lambda i: (i, gate_block + 1))],
        out_specs=pl.BlockSpec((tm, n), row),
        out_shape=jax.ShapeDtypeStruct((t, n), BF16),
        compiler_params=_params("parallel"),
        name="mix",
    )(oa, ob, wg, ws, main, main)


def _layer(x2, positions, batch, seq, ln1_w, w_in, conv_w, a_log, dt_bias, gdn_norm_w, w_gdn_proj,
           q_norm_w, k_norm_w, w_swa_proj, w_out, ln2_w, w_ff1, w_ff2):
    d_model = x2.shape[1]
    gw3 = 3 * GDN_WIDTH
    o_a, o_b = gw3 + GDN_WIDTH, gw3 + GDN_WIDTH + GDN_HEADS
    o_swa = o_b + GDN_HEADS
    o_ga = o_swa + 3 * SWA_WIDTH
    assert seq % SWA_TOKENS == 0 and seq % GDN_BLOCK == 0
    assert all(window // dl == SWA_BLOCK and SWA_TOKENS % (dl * SWA_BLOCK) == 0 for window, dl in SWA_GROUPS)
    swa_cols = []
    for j in range(SWA_HEADS_PER_GROUP):
        for g in range(len(SWA_GROUPS)):
            for part in range(3):
                lo = o_swa + part * SWA_WIDTH + (g * SWA_HEADS_PER_GROUP + j) * HEAD_DIM
                swa_cols.append(w_in[:, lo:lo + HEAD_DIM])
    w_main = jnp.concatenate([w_in[:, :o_a], w_in[:, o_ga:]], axis=1).astype(BF16)
    w_swa = jnp.concatenate(swa_cols, axis=1).astype(BF16)
    w_ab = jnp.pad(w_in[:, o_a:o_swa], ((0, 0), (0, LANES - 2 * GDN_HEADS))).astype(BF16)
    ln1 = ln1_w.reshape(1, -1)
    main, ab = _norm_matmul(x2, ln1, w_main, tm=1024, tn=2048, name="in_proj_main", w_ab=w_ab)
    swa_in = _norm_matmul(x2, ln1, w_swa, tm=1024, tn=SWA_WIDTH, name="in_proj_swa")

    pad8 = lambda v: jnp.pad(v.astype(F32), (0, LANES - v.shape[0])).reshape(1, LANES)
    o_a_branch = _gdn(main, ab, conv_w.astype(F32), pad8(a_log), pad8(dt_bias),
                      gdn_norm_w.reshape(1, -1).astype(F32), batch, seq)

    half = HEAD_DIM // 2
    inv_freq = ROPE_THETA ** (-jnp.arange(half, dtype=F32) / half)
    cos_t, sin_t = _rope_tables(positions.reshape(-1, 1), jnp.concatenate([inv_freq, inv_freq]).reshape(1, HEAD_DIM))
    o_b_branch = _swa(swa_in, cos_t, sin_t, q_norm_w.reshape(1, -1).astype(F32),
                      k_norm_w.reshape(1, -1).astype(F32), batch, seq)

    mixed = _mix(o_a_branch, o_b_branch, w_gdn_proj.astype(BF16), w_swa_proj.astype(BF16), main,
                 gate_block=4 * GDN_WIDTH // d_model, tm=512)
    x2 = _residual_matmul(mixed, w_out.astype(BF16), x2, tm=512, tk=d_model, name="out_proj")
    u2 = _norm_matmul(x2, ln2_w.reshape(1, -1), w_ff1.astype(BF16), tm=1024, tn=2048, name="ff1", square_relu=True)
    return _residual_matmul(u2, w_ff2.astype(BF16), x2, tm=1024, tk=1024, name="ff2")


def kernel(x, positions, ln1_w, w_in, gdn_conv_w, gdn_a_log, gdn_dt_bias, gdn_norm_w, w_gdn_proj,
           swa_q_norm_w, swa_k_norm_w, w_swa_proj, w_out, ln2_w, w_ff1, w_ff2):
    batch, seq, d_model = x.shape
    x2 = x.reshape(batch * seq, d_model)
    for l in range(ln1_w.shape[0]):
        x2 = _layer(x2, positions, batch, seq, ln1_w[l], w_in[l], gdn_conv_w[l], gdn_a_log[l], gdn_dt_bias[l],
                    gdn_norm_w[l], w_gdn_proj[l], swa_q_norm_w[l], swa_k_norm_w[l], w_swa_proj[l], w_out[l],
                    ln2_w[l], w_ff1[l], w_ff2[l])
    return x2.reshape(batch, seq, d_model)
```

```python
import functools

import jax
import jax.numpy as jnp
from jax import lax
from jax.experimental import pallas as pl
from jax.experimental.pallas import tpu as pltpu

F32 = jnp.float32
BF16 = jnp.bfloat16

HEAD_DIM = 128
EPS = 1e-6
GDN_HEADS = 8
GDN_WIDTH = GDN_HEADS * HEAD_DIM
GDN_CONV = 4
GDN_CHUNK = 64
SWA_GROUPS = ((128, 1), (512, 4), (2048, 16))
SWA_HEADS_PER_GROUP = 4
SWA_GROUP_WIDTH = SWA_HEADS_PER_GROUP * HEAD_DIM
SWA_WIDTH = len(SWA_GROUPS) * SWA_GROUP_WIDTH
SWA_BLOCK = 128
ROPE_THETA = 10000.0
NEG_INF = -1e30

VMEM_LIMIT_BYTES = 52 * 1024 * 1024
LANES = 128
SUBLANES = 8
CONV_TAIL_ROWS = SUBLANES
NORM_ROWS = 256
GDN_BLOCK = 256
SWA_TOKENS = 2048


def _params(*sem):
    return pltpu.CompilerParams(dimension_semantics=sem, vmem_limit_bytes=VMEM_LIMIT_BYTES)


def _sigmoid(x):
    return 0.5 * jnp.tanh(0.5 * x) + 0.5


def _silu(x):
    h = 0.5 * x
    return h * jnp.tanh(h) + h


def _rope_table_kernel(pos_ref, invf_ref, cos_ref, sin_ref):
    ang = pos_ref[...].astype(F32) * invf_ref[...]
    cos_ref[...] = jnp.cos(ang)
    lane = lax.broadcasted_iota(jnp.int32, ang.shape, 1)
    s = jnp.sin(ang)
    sin_ref[...] = jnp.where(lane < HEAD_DIM // 2, -s, s)


def _rope_tables(pos_col, inv_freq2):
    t = pos_col.shape[0]
    tm = min(t, 2048)
    return pl.pallas_call(
        _rope_table_kernel,
        grid=(t // tm,),
        in_specs=[pl.BlockSpec((tm, 1), lambda i: (i, 0)),
                  pl.BlockSpec((1, HEAD_DIM), lambda i: (0, 0))],
        out_specs=[pl.BlockSpec((tm, HEAD_DIM), lambda i: (i, 0)),
                   pl.BlockSpec((tm, HEAD_DIM), lambda i: (i, 0))],
        out_shape=[jax.ShapeDtypeStruct((t, HEAD_DIM), F32)] * 2,
        compiler_params=_params("parallel"),
        name="rope_table",
    )(pos_col, inv_freq2)


def _rmsnorm_to(h_ref, x_ref, lnw_ref):
    tm = x_ref.shape[0]
    w = lnw_ref[...]

    def body(c, carry):
        rows = pl.ds(pl.multiple_of(c * NORM_ROWS, NORM_ROWS), NORM_ROWS)
        x = x_ref[rows, :]
        y = x * lax.rsqrt(jnp.mean(x * x, axis=-1, keepdims=True) + EPS)
        h_ref[rows, :] = (y * w).astype(BF16)
        return carry

    lax.fori_loop(0, tm // NORM_ROWS, body, 0)


def _norm_matmul_kernel(x_ref, lnw_ref, w_ref, *rest, with_ab, square_relu):
    if with_ab:
        wab_ref, o_ref, ab_ref, h_ref = rest
    else:
        o_ref, h_ref = rest

    @pl.when(pl.program_id(1) == 0)
    def _():
        _rmsnorm_to(h_ref, x_ref, lnw_ref)
        if with_ab:
            ab_ref[...] = jnp.dot(h_ref[...], wab_ref[...], preferred_element_type=F32)

    acc = jnp.dot(h_ref[...], w_ref[...], preferred_element_type=F32)
    if square_relu:
        acc = jnp.maximum(acc, 0.0)
        acc = acc * acc
    o_ref[...] = acc.astype(o_ref.dtype)


def _norm_matmul(x2, ln_w, w, tm, tn, name, w_ab=None, square_relu=False):
    t, k = x2.shape
    n = w.shape[1]
    in_specs = [pl.BlockSpec((tm, k), lambda i, j: (i, 0)),
                pl.BlockSpec((1, k), lambda i, j: (0, 0)),
                pl.BlockSpec((k, tn), lambda i, j: (0, j))]
    out_specs = [pl.BlockSpec((tm, tn), lambda i, j: (i, j))]
    out_shape = [jax.ShapeDtypeStruct((t, n), BF16)]
    args = [x2, ln_w, w]
    if w_ab is not None:
        in_specs.append(pl.BlockSpec((k, LANES), lambda i, j: (0, 0)))
        out_specs.append(pl.BlockSpec((tm, LANES), lambda i, j: (i, 0)))
        out_shape.append(jax.ShapeDtypeStruct((t, LANES), F32))
        args.append(w_ab)
    out = pl.pallas_call(
        functools.partial(_norm_matmul_kernel, with_ab=w_ab is not None, square_relu=square_relu),
        grid=(t // tm, n // tn),
        in_specs=in_specs, out_specs=out_specs, out_shape=out_shape,
        scratch_shapes=[pltpu.VMEM((tm, k), BF16)],
        compiler_params=_params("parallel", "arbitrary"),
        name=name,
    )(*args)
    return out if w_ab is not None else out[0]


def _swa_proj_kernel(x_ref, lnw_ref, w_ref, cos_ref, sin_ref, qnw_ref, knw_ref, o_ref, h_ref, acc_ref, *, n_slots):
    d = HEAD_DIM
    step = pl.program_id(0)

    @pl.when(step == 0)
    def _():
        acc_ref[...] = jnp.zeros(acc_ref.shape, F32)

    @pl.when(step % n_slots == 0)
    def _():
        _rmsnorm_to(h_ref, x_ref, lnw_ref)

    cos, sin = cos_ref[...], sin_ref[...]
    norm_w = (qnw_ref[...] * (d ** -0.5), knw_ref[...])
    for g in range(acc_ref.shape[1] // d):
        y = acc_ref[:, g * d:(g + 1) * d]
        if g % 3 < 2:
            y = y * lax.rsqrt(jnp.mean(y * y, axis=-1, keepdims=True) + EPS) * norm_w[g % 3]
            y = y * cos + pltpu.roll(y, d // 2, 1) * sin
        o_ref[:, g * d:(g + 1) * d] = y.astype(o_ref.dtype)
    acc_ref[...] = jnp.dot(h_ref[...], w_ref[...], preferred_element_type=F32)


def _swa_proj(x2, ln_w, w, cos_t, sin_t, qnw, knw, tm):
    t, k = x2.shape
    n = w.shape[1]
    tn = 3 * len(SWA_GROUPS) * HEAD_DIM
    n_slots = n // tn
    n_tiles = (t // tm) * n_slots
    cur = lambda s: jnp.minimum(s, n_tiles - 1)
    prev = lambda s: jnp.maximum(s - 1, 0)
    const = lambda s: (0, 0)
    return pl.pallas_call(
        functools.partial(_swa_proj_kernel, n_slots=n_slots),
        grid=(n_tiles + 1,),
        in_specs=[pl.BlockSpec((tm, k), lambda s: (cur(s) // n_slots, 0)),
                  pl.BlockSpec((1, k), const),
                  pl.BlockSpec((k, tn), lambda s: (0, cur(s) % n_slots)),
                  pl.BlockSpec((tm, HEAD_DIM), lambda s: (prev(s) // n_slots, 0)),
                  pl.BlockSpec((tm, HEAD_DIM), lambda s: (prev(s) // n_slots, 0)),
                  pl.BlockSpec((1, HEAD_DIM), const), pl.BlockSpec((1, HEAD_DIM), const)],
        out_specs=pl.BlockSpec((tm, tn), lambda s: (prev(s) // n_slots, prev(s) % n_slots)),
        out_shape=jax.ShapeDtypeStruct((t, n), BF16),
        scratch_shapes=[pltpu.VMEM((tm, k), BF16), pltpu.VMEM((tm, tn), F32)],
        compiler_params=_params("arbitrary"),
        name="in_proj_swa",
    )(x2, ln_w, w, cos_t, sin_t, qnw, knw)


def _residual_matmul_kernel(a_ref, w_ref, r_ref, o_ref):
    @pl.when(pl.program_id(1) == 0)
    def _():
        o_ref[...] = r_ref[...]

    o_ref[...] += jnp.dot(a_ref[...], w_ref[...], preferred_element_type=F32)


def _residual_matmul(a, w, resid, tm, tk, name):
    t, k = a.shape
    n = w.shape[1]
    return pl.pallas_call(
        _residual_matmul_kernel,
        grid=(t // tm, k // tk),
        in_specs=[pl.BlockSpec((tm, tk), lambda i, kk: (i, kk)),
                  pl.BlockSpec((tk, n), lambda i, kk: (kk, 0)),
                  pl.BlockSpec((tm, n), lambda i, kk: (i, 0))],
        out_specs=pl.BlockSpec((tm, n), lambda i, kk: (i, 0)),
        out_shape=jax.ShapeDtypeStruct((t, n), F32),
        compiler_params=_params("parallel", "arbitrary"),
        name=name,
    )(a, w, resid)


def _gdn_kernel(qkvz_ref, ab_ref, convw_ref, alog_ref, dtb_ref, normw_ref, o_ref,
                xs_ref, act_ref, kt_ref, g_ref, gt_ref, beta_ref, state_ref):
    nbatch, tb = qkvz_ref.shape[0], qkvz_ref.shape[1]
    c_len, d, nh, w = GDN_CHUNK, HEAD_DIM, GDN_HEADS, GDN_WIDTH
    n_chunks = tb // c_len

    @pl.when(pl.program_id(0) == 0)
    def _():
        xs_ref[:, 0:CONV_TAIL_ROWS, :] = jnp.zeros((nbatch, CONV_TAIL_ROWS, 3 * w), F32)
        state_ref[...] = jnp.zeros(state_ref.shape, F32)

    ri = lax.broadcasted_iota(jnp.int32, (tb, tb), 0)
    ci = lax.broadcasted_iota(jnp.int32, (tb, tb), 1)
    tri = ((ci <= ri) & (ri // c_len == ci // c_len)).astype(F32)

    for b in range(nbatch):
        xs_ref[b, CONV_TAIL_ROWS:CONV_TAIL_ROWS + tb, :] = qkvz_ref[b, :, 0:3 * w].astype(F32)
        for cg in range(3 * nh):
            cols = slice(cg * d, (cg + 1) * d)
            acc = (xs_ref[b, CONV_TAIL_ROWS:CONV_TAIL_ROWS + tb, cols]
                   * convw_ref[GDN_CONV - 1:GDN_CONV, cols])
            for s in range(1, GDN_CONV):
                acc = acc + (xs_ref[b, CONV_TAIL_ROWS - s:CONV_TAIL_ROWS - s + tb, cols]
                             * convw_ref[GDN_CONV - 1 - s:GDN_CONV - s, cols])
            y = _silu(acc)
            if cg < 2 * nh:
                y = y * lax.rsqrt(jnp.sum(y * y, axis=-1, keepdims=True) + EPS)
                if cg < nh:
                    y = y * (d ** -0.5)
            act_ref[b, :, cols] = y
            if nh <= cg < 2 * nh:
                for c in range(n_chunks):
                    kt_ref[b, c, cg - nh] = y[c * c_len:(c + 1) * c_len, :].T
        xs_ref[b, 0:CONV_TAIL_ROWS, :] = xs_ref[b, tb:tb + CONV_TAIL_ROWS, :]

        ab = ab_ref[b]
        beta_ref[b] = _sigmoid(ab)
        z = ab + dtb_ref[...]
        softplus = jnp.maximum(z, 0.0) + jnp.log1p(jnp.exp(-jnp.abs(z)))
        g = -jnp.exp(alog_ref[...]) * softplus
        g_cum = jnp.dot(tri, g, precision=lax.Precision.HIGHEST, preferred_element_type=F32)
        g_ref[b] = g_cum
        for c in range(n_chunks):
            gt_ref[b, c] = g_cum[c * c_len:(c + 1) * c_len, :].T

    ii = lax.broadcasted_iota(jnp.int32, (c_len, c_len), 0)
    jj = lax.broadcasted_iota(jnp.int32, (c_len, c_len), 1)
    incl = ii >= jj
    strict = ii > jj
    norm_w = normw_ref[...]
    chains = [(b, h) for b in range(nbatch) for h in range(nh)]

    def mm(a, b):
        return jnp.dot(a.astype(BF16), b.astype(BF16), preferred_element_type=F32)

    def chunk_body(c, carry):
        rows = slice(c * c_len, (c + 1) * c_len)
        g_all = [g_ref[b, rows, :] for b in range(nbatch)]
        beta_all = [beta_ref[b, rows, :] for b in range(nbatch)]
        gt_all = [gt_ref[b, c] for b in range(nbatch)]

        st = []
        for b, h in chains:
            q = act_ref[b, rows, h * d:(h + 1) * d]
            k = act_ref[b, rows, w + h * d:w + (h + 1) * d]
            v = act_ref[b, rows, 2 * w + h * d:2 * w + (h + 1) * d]
            kt = kt_ref[b, c, h]
            g_col = g_all[b][:, h:h + 1]
            b_col = beta_all[b][:, nh + h:nh + h + 1]
            g_row = gt_all[b][h:h + 1, :]
            g_last = g_row[:, c_len - 1:c_len]
            decay = jnp.where(incl, jnp.exp(jnp.where(incl, g_col - g_row, 0.0)), 0.0)
            e_g = jnp.exp(g_col)
            kb = k * b_col
            st.append(dict(q=q, kt=kt, g_row=g_row, g_last=g_last, decay=decay, kb=kb,
                           qe=q * e_g, rhs=jnp.concatenate([v * b_col, kb * e_g], axis=1)))
        for s in st:
            s1 = mm(jnp.concatenate([s["kb"], s["q"]], axis=0), s["kt"])
            s["a"] = jnp.where(strict, s1[:c_len] * s["decay"], 0.0)
            s["a_qk"] = jnp.where(incl, s1[c_len:] * s["decay"], 0.0)
        for s in st:
            s["x"] = s["rhs"] - mm(s["a"], s["rhs"])
            s["m"] = mm(s["a"], s["a"])
        for it in range(5):
            for s in st:
                s["x"] = s["x"] + mm(s["m"], s["x"])
                if it < 4:
                    s["m"] = mm(s["m"], s["m"])
        for (b, h), s in zip(chains, st):
            s["state"] = state_ref[b, h]
            s["ws"] = mm(jnp.concatenate([s["x"][:, d:], s["qe"]], axis=0), s["state"])
        for (b, h), s in zip(chains, st):
            v_new = s["x"][:, :d] - s["ws"][:c_len]
            o = s["ws"][c_len:] + mm(s["a_qk"], v_new)
            kdt = s["kt"] * jnp.exp(s["g_last"] - s["g_row"])
            state_ref[b, h] = s["state"] * jnp.exp(s["g_last"]) + mm(kdt, v_new)
            o = o * lax.rsqrt(jnp.mean(o * o, axis=-1, keepdims=True) + EPS)
            zh = qkvz_ref[b, rows, 3 * w + h * d:3 * w + (h + 1) * d].astype(F32)
            o_ref[b, rows, h * d:(h + 1) * d] = (o * norm_w * _silu(zh)).astype(o_ref.dtype)
        return carry

    for c in range(n_chunks):
        chunk_body(c, 0)


def _gdn(main, ab, conv_w, a_log, dt_bias, norm_w, batch, seq):
    tb = GDN_BLOCK
    w = GDN_WIDTH
    n_chunks = tb // GDN_CHUNK
    blk = lambda s: (0, s, 0)
    const = lambda s: (0, 0)
    out = pl.pallas_call(
        _gdn_kernel,
        grid=(seq // tb,),
        in_specs=[pl.BlockSpec((batch, tb, 4 * w), blk),
                  pl.BlockSpec((batch, tb, LANES), blk),
                  pl.BlockSpec((GDN_CONV, 3 * w), const),
                  pl.BlockSpec((1, LANES), const),
                  pl.BlockSpec((1, LANES), const),
                  pl.BlockSpec((1, HEAD_DIM), const)],
        out_specs=pl.BlockSpec((batch, tb, w), blk),
        out_shape=jax.ShapeDtypeStruct((batch, seq, w), BF16),
        scratch_shapes=[pltpu.VMEM((batch, tb + CONV_TAIL_ROWS, 3 * w), F32),
                        pltpu.VMEM((batch, tb, 3 * w), F32),
                        pltpu.VMEM((batch, n_chunks, GDN_HEADS, HEAD_DIM, GDN_CHUNK), F32),
                        pltpu.VMEM((batch, tb, LANES), F32),
                        pltpu.VMEM((batch, n_chunks, LANES, GDN_CHUNK), F32),
                        pltpu.VMEM((batch, tb, LANES), F32),
                        pltpu.VMEM((batch, GDN_HEADS, HEAD_DIM, HEAD_DIM), F32)],
        compiler_params=_params("arbitrary"),
        name="gdn",
    )(main.reshape(batch, seq, main.shape[1]), ab.reshape(batch, seq, LANES), conv_w, a_log, dt_bias, norm_w)
    return out.reshape(batch * seq, w)


def _swa_kernel(x_ref, o_ref,
                q_s, k_s0, k_s1, k_s2, v_s0, v_s1, v_s2, og_s, lg_s):
    tq, d, blk = x_ref.shape[0], HEAD_DIM, SWA_BLOCK
    first = pl.program_id(2) == 0
    k_ss, v_ss = (k_s0, k_s1, k_s2), (v_s0, v_s1, v_s2)
    qi = lax.broadcasted_iota(jnp.int32, (blk, 2 * blk), 0)
    cj = lax.broadcasted_iota(jnp.int32, (blk, 2 * blk), 1)
    band = (cj >= qi) & (cj <= qi + blk)
    band_first = band & (cj >= jnp.where(first, blk, 0))
    ones_v = jnp.ones((2 * blk, d), BF16)

    for g, (window, dl) in enumerate(SWA_GROUPS):
        tail = blk * dl
        k_s, v_s = k_ss[g], v_ss[g]

        @pl.when(first)
        def _(k_s=k_s, v_s=v_s, tail=tail):
            k_s[0:tail, :] = jnp.zeros((tail, d), F32)
            v_s[0:tail, :] = jnp.zeros((tail, d), F32)

        base = 3 * g * d
        q_s[...] = x_ref[:, base:base + d].astype(F32)
        k_s[tail:tail + tq, :] = x_ref[:, base + d:base + 2 * d].astype(F32)
        v_s[tail:tail + tq, :] = x_ref[:, base + 2 * d:base + 3 * d].astype(F32)
        for r in range(dl):
            for i in range(tq // tail):
                start = tail * i + r
                if dl == 1:
                    qrows, krows = pl.ds(start, blk), pl.ds(start, 2 * blk)
                else:
                    qrows, krows = pl.ds(start, blk, stride=dl), pl.ds(start, 2 * blk, stride=dl)
                q = q_s[qrows, :].astype(BF16)
                k = k_s[krows, :].astype(BF16)
                v = jnp.concatenate([v_s[krows, :].astype(BF16), ones_v], axis=1)
                s = lax.dot_general(q, k, (((1,), (1,)), ((), ())), preferred_element_type=F32)
                s = jnp.where(band_first if i == 0 else band, s, NEG_INF)
                mx = jnp.max(s, axis=-1, keepdims=True)
                p = jnp.exp(s - mx)
                pv = jnp.dot(p.astype(BF16), v, preferred_element_type=F32)
                den = pv[:, d:]
                og_s[g, qrows, :] = pv[:, :d] / den
                lg_s[g, qrows, :] = mx + jnp.log(den)
        k_s[0:tail, :] = k_s[tq:tq + tail, :]
        v_s[0:tail, :] = v_s[tq:tq + tail, :]

    l0, l1, l2 = lg_s[0], lg_s[1], lg_s[2]
    mx = jnp.maximum(jnp.maximum(l0, l1), l2)
    e0, e1, e2 = jnp.exp(l0 - mx), jnp.exp(l1 - mx), jnp.exp(l2 - mx)
    o_ref[...] = ((og_s[0] * e0 + og_s[1] * e1 + og_s[2] * e2) / (e0 + e1 + e2)).astype(o_ref.dtype)


def _swa(xs, batch, seq):
    t = xs.shape[0]
    d = HEAD_DIM
    tq = SWA_TOKENS
    nb = seq // tq
    slot = lambda b, j, n: (b * nb + n, j)
    dils = [dl for _, dl in SWA_GROUPS]
    return pl.pallas_call(
        _swa_kernel,
        grid=(batch, SWA_HEADS_PER_GROUP, nb),
        in_specs=[pl.BlockSpec((tq, 3 * len(SWA_GROUPS) * d), slot)],
        out_specs=pl.BlockSpec((tq, d), slot),
        out_shape=jax.ShapeDtypeStruct((t, SWA_GROUP_WIDTH), BF16),
        scratch_shapes=[pltpu.VMEM((tq, d), F32)]
        + [pltpu.VMEM((SWA_BLOCK * dl + tq, d), F32) for dl in dils] * 2
        + [pltpu.VMEM((len(SWA_GROUPS), tq, d), F32)] * 2,
        compiler_params=_params("parallel", "parallel", "arbitrary"),
        name="swa",
    )(xs)


def _mix_kernel(oa_ref, ob_ref, wg_ref, ws_ref, ga_ref, gb_ref, out_ref):
    ya = jnp.dot(oa_ref[...], wg_ref[...], preferred_element_type=F32)
    yb = jnp.dot(ob_ref[...], ws_ref[...], preferred_element_type=F32)
    mixed = _sigmoid(ga_ref[...].astype(F32)) * ya + _sigmoid(gb_ref[...].astype(F32)) * yb
    out_ref[...] = mixed.astype(out_ref.dtype)


def _mix(oa, ob, wg, ws, main, gate_block, tm):
    t = oa.shape[0]
    n = wg.shape[1]
    row = lambda i: (i, 0)
    const = lambda i: (0, 0)
    return pl.pallas_call(
        _mix_kernel,
        grid=(t // tm,),
        in_specs=[pl.BlockSpec((tm, oa.shape[1]), row), pl.BlockSpec((tm, ob.shape[1]), row),
                  pl.BlockSpec(wg.shape, const), pl.BlockSpec(ws.shape, const),
                  pl.BlockSpec((tm, n), lambda i: (i, gate_block)),
                  pl.BlockSpec((tm, n), lambda i: (i, gate_block + 1))],
        out_specs=pl.BlockSpec((tm, n), row),
        out_shape=jax.ShapeDtypeStruct((t, n), BF16),
        compiler_params=_params("parallel"),
        name="mix",
    )(oa, ob, wg, ws, main, main)


def _layer(x2, positions, batch, seq, ln1_w, w_in, conv_w, a_log, dt_bias, gdn_norm_w, w_gdn_proj,
           q_norm_w, k_norm_w, w_swa_proj, w_out, ln2_w, w_ff1, w_ff2):
    d_model = x2.shape[1]
    gw3 = 3 * GDN_WIDTH
    o_a, o_b = gw3 + GDN_WIDTH, gw3 + GDN_WIDTH + GDN_HEADS
    o_swa = o_b + GDN_HEADS
    o_ga = o_swa + 3 * SWA_WIDTH
    assert seq % SWA_TOKENS == 0 and seq % GDN_BLOCK == 0
    assert all(window // dl == SWA_BLOCK and SWA_TOKENS % (dl * SWA_BLOCK) == 0 for window, dl in SWA_GROUPS)
    swa_cols = []
    for j in range(SWA_HEADS_PER_GROUP):
        for g in range(len(SWA_GROUPS)):
            for part in range(3):
                lo = o_swa + part * SWA_WIDTH + (g * SWA_HEADS_PER_GROUP + j) * HEAD_DIM
                swa_cols.append(w_in[:, lo:lo + HEAD_DIM])
    w_main = jnp.concatenate([w_in[:, :o_a], w_in[:, o_ga:]], axis=1).astype(BF16)
    w_swa = jnp.concatenate(swa_cols, axis=1).astype(BF16)
    w_ab = jnp.pad(w_in[:, o_a:o_swa], ((0, 0), (0, LANES - 2 * GDN_HEADS))).astype(BF16)
    ln1 = ln1_w.reshape(1, -1)
    main, ab = _norm_matmul(x2, ln1, w_main, tm=1024, tn=2048, name="in_proj_main", w_ab=w_ab)

    pad8 = lambda v: jnp.pad(v.astype(F32), (0, LANES - v.shape[0])).reshape(1, LANES)
    o_a_branch = _gdn(main, ab, conv_w.astype(F32), pad8(a_log), pad8(dt_bias),
                      gdn_norm_w.reshape(1, -1).astype(F32), batch, seq)

    half = HEAD_DIM // 2
    inv_freq = ROPE_THETA ** (-jnp.arange(half, dtype=F32) / half)
    cos_t, sin_t = _rope_tables(positions.reshape(-1, 1), jnp.concatenate([inv_freq, inv_freq]).reshape(1, HEAD_DIM))
    swa_in = _swa_proj(x2, ln1, w_swa, cos_t, sin_t, q_norm_w.reshape(1, -1).astype(F32),
                       k_norm_w.reshape(1, -1).astype(F32), tm=1024)
    o_b_branch = _swa(swa_in, batch, seq)

    mixed = _mix(o_a_branch, o_b_branch, w_gdn_proj.astype(BF16), w_swa_proj.astype(BF16), main,
                 gate_block=4 * GDN_WIDTH // d_model, tm=512)
    x2 = _residual_matmul(mixed, w_out.astype(BF16), x2, tm=512, tk=d_model, name="out_proj")
    u2 = _norm_matmul(x2, ln2_w.reshape(1, -1), w_ff1.astype(BF16), tm=1024, tn=2048, name="ff1", square_relu=True)
    return _residual_matmul(u2, w_ff2.astype(BF16), x2, tm=1024, tk=1024, name="ff2")


def kernel(x, positions, ln1_w, w_in, gdn_conv_w, gdn_a_log, gdn_dt_bias, gdn_norm_w, w_gdn_proj,
           swa_q_norm_w, swa_k_norm_w, w_swa_proj, w_out, ln2_w, w_ff1, w_ff2):
    batch, seq, d_model = x.shape
    x2 = x.reshape(batch * seq, d_model)
    for l in range(ln1_w.shape[0]):
        x2 = _layer(x2, positions, batch, seq, ln1_w[l], w_in[l], gdn_conv_w[l], gdn_a_log[l], gdn_dt_bias[l],
                    gdn_norm_w[l], w_gdn_proj[l], swa_q_norm_w[l], swa_k_norm_w[l], w_swa_proj[l], w_out[l],
                    ln2_w[l], w_ff1[l], w_ff2[l])
    return x2.reshape(batch, seq, d_model)
```

```python
import functools

import jax
import jax.numpy as jnp
from jax import lax
from jax.experimental import pallas as pl
from jax.experimental.pallas import tpu as pltpu

F32 = jnp.float32
BF16 = jnp.bfloat16

HEAD_DIM = 128
EPS = 1e-6
GDN_HEADS = 8
GDN_WIDTH = GDN_HEADS * HEAD_DIM
GDN_CONV = 4
GDN_CHUNK = 64
SWA_GROUPS = ((128, 1), (512, 4), (2048, 16))
SWA_HEADS_PER_GROUP = 4
SWA_GROUP_WIDTH = SWA_HEADS_PER_GROUP * HEAD_DIM
SWA_WIDTH = len(SWA_GROUPS) * SWA_GROUP_WIDTH
SWA_BLOCK = 128
ROPE_THETA = 10000.0
NEG_INF = -1e30

VMEM_LIMIT_BYTES = 56 * 1024 * 1024
LANES = 128
SUBLANES = 8
CONV_TAIL_ROWS = SUBLANES
NORM_ROWS = 256
GDN_BLOCK = 256
SWA_TOKENS = 2048
SWA_PROJ_TILE_N = 2304


def _params(*sem):
    return pltpu.CompilerParams(dimension_semantics=sem, vmem_limit_bytes=VMEM_LIMIT_BYTES)


def _sigmoid(x):
    return 0.5 * jnp.tanh(0.5 * x) + 0.5


def _silu(x):
    h = 0.5 * x
    return h * jnp.tanh(h) + h


def _rope_table_kernel(pos_ref, invf_ref, cos_ref, sin_ref):
    ang = pos_ref[...].astype(F32) * invf_ref[...]
    cos_ref[...] = jnp.cos(ang)
    lane = lax.broadcasted_iota(jnp.int32, ang.shape, 1)
    s = jnp.sin(ang)
    sin_ref[...] = jnp.where(lane < HEAD_DIM // 2, -s, s)


def _rope_tables(pos_col, inv_freq2):
    t = pos_col.shape[0]
    tm = min(t, 2048)
    return pl.pallas_call(
        _rope_table_kernel,
        grid=(t // tm,),
        in_specs=[pl.BlockSpec((tm, 1), lambda i: (i, 0)),
                  pl.BlockSpec((1, HEAD_DIM), lambda i: (0, 0))],
        out_specs=[pl.BlockSpec((tm, HEAD_DIM), lambda i: (i, 0)),
                   pl.BlockSpec((tm, HEAD_DIM), lambda i: (i, 0))],
        out_shape=[jax.ShapeDtypeStruct((t, HEAD_DIM), F32)] * 2,
        compiler_params=_params("parallel"),
        name="rope_table",
    )(pos_col, inv_freq2)


def _rmsnorm_to(h_ref, x_ref, lnw_ref):
    tm = x_ref.shape[0]
    w = lnw_ref[...]

    def body(c, carry):
        rows = pl.ds(pl.multiple_of(c * NORM_ROWS, NORM_ROWS), NORM_ROWS)
        x = x_ref[rows, :]
        y = x * lax.rsqrt(jnp.mean(x * x, axis=-1, keepdims=True) + EPS)
        h_ref[rows, :] = (y * w).astype(BF16)
        return carry

    lax.fori_loop(0, tm // NORM_ROWS, body, 0)


def _norm_matmul_kernel(x_ref, lnw_ref, w_ref, *rest, with_ab, square_relu):
    if with_ab:
        wab_ref, o_ref, ab_ref, h_ref = rest
    else:
        o_ref, h_ref = rest

    @pl.when(pl.program_id(1) == 0)
    def _():
        _rmsnorm_to(h_ref, x_ref, lnw_ref)
        if with_ab:
            ab_ref[...] = jnp.dot(h_ref[...], wab_ref[...], preferred_element_type=F32)

    acc = jnp.dot(h_ref[...], w_ref[...], preferred_element_type=F32)
    if square_relu:
        acc = jnp.maximum(acc, 0.0)
        acc = acc * acc
    o_ref[...] = acc.astype(o_ref.dtype)


def _norm_matmul(x2, ln_w, w, tm, tn, name, w_ab=None, square_relu=False):
    t, k = x2.shape
    n = w.shape[1]
    in_specs = [pl.BlockSpec((tm, k), lambda i, j: (i, 0)),
                pl.BlockSpec((1, k), lambda i, j: (0, 0)),
                pl.BlockSpec((k, tn), lambda i, j: (0, j))]
    out_specs = [pl.BlockSpec((tm, tn), lambda i, j: (i, j))]
    out_shape = [jax.ShapeDtypeStruct((t, n), BF16)]
    args = [x2, ln_w, w]
    if w_ab is not None:
        in_specs.append(pl.BlockSpec((k, LANES), lambda i, j: (0, 0)))
        out_specs += [pl.BlockSpec((tm, LANES), lambda i, j: (i, 0)), pl.BlockSpec((tm, k), lambda i, j: (i, 0))]
        out_shape += [jax.ShapeDtypeStruct((t, LANES), F32), jax.ShapeDtypeStruct((t, k), BF16)]
        args.append(w_ab)
    out = pl.pallas_call(
        functools.partial(_norm_matmul_kernel, with_ab=w_ab is not None, square_relu=square_relu),
        grid=(t // tm, n // tn),
        in_specs=in_specs, out_specs=out_specs, out_shape=out_shape,
        scratch_shapes=[] if w_ab is not None else [pltpu.VMEM((tm, k), BF16)],
        compiler_params=_params("parallel", "arbitrary"),
        name=name,
    )(*args)
    return out if w_ab is not None else out[0]


def _swa_proj_kernel(h_ref, w_ref, cos_ref, sin_ref, qnw_ref, knw_ref, o_ref, acc_ref):
    d = HEAD_DIM

    @pl.when(pl.program_id(0) == 0)
    def _():
        acc_ref[...] = jnp.zeros(acc_ref.shape, F32)

    cos, sin = cos_ref[...], sin_ref[...]
    norm_w = (qnw_ref[...] * (d ** -0.5), knw_ref[...])
    for g in range(acc_ref.shape[1] // d):
        y = acc_ref[:, g * d:(g + 1) * d]
        if g % 3 < 2:
            y = y * lax.rsqrt(jnp.mean(y * y, axis=-1, keepdims=True) + EPS) * norm_w[g % 3]
            y = y * cos + pltpu.roll(y, d // 2, 1) * sin
        o_ref[:, g * d:(g + 1) * d] = y.astype(o_ref.dtype)
    acc_ref[...] = jnp.dot(h_ref[...], w_ref[...], preferred_element_type=F32)


def _swa_proj(h, w, cos_t, sin_t, qnw, knw, tm, tn):
    t, k = h.shape
    n = w.shape[1]
    nj = n // tn
    n_tiles = (t // tm) * nj
    cur = lambda s: jnp.minimum(s, n_tiles - 1)
    prev = lambda s: jnp.maximum(s - 1, 0)
    const = lambda s: (0, 0)
    return pl.pallas_call(
        _swa_proj_kernel,
        grid=(n_tiles + 1,),
        in_specs=[pl.BlockSpec((tm, k), lambda s: (cur(s) // nj, 0)),
                  pl.BlockSpec((k, tn), lambda s: (0, cur(s) % nj)),
                  pl.BlockSpec((tm, HEAD_DIM), lambda s: (prev(s) // nj, 0)),
                  pl.BlockSpec((tm, HEAD_DIM), lambda s: (prev(s) // nj, 0)),
                  pl.BlockSpec((1, HEAD_DIM), const), pl.BlockSpec((1, HEAD_DIM), const)],
        out_specs=pl.BlockSpec((tm, tn), lambda s: (prev(s) // nj, prev(s) % nj)),
        out_shape=jax.ShapeDtypeStruct((t, n), BF16),
        scratch_shapes=[pltpu.VMEM((tm, tn), F32)],
        compiler_params=_params("arbitrary"),
        name="in_proj_swa",
    )(h, w, cos_t, sin_t, qnw, knw)


def _residual_matmul_kernel(a_ref, w_ref, r_ref, o_ref):
    @pl.when(pl.program_id(1) == 0)
    def _():
        o_ref[...] = r_ref[...]

    o_ref[...] += jnp.dot(a_ref[...], w_ref[...], preferred_element_type=F32)


def _residual_matmul(a, w, resid, tm, tk, name):
    t, k = a.shape
    n = w.shape[1]
    return pl.pallas_call(
        _residual_matmul_kernel,
        grid=(t // tm, k // tk),
        in_specs=[pl.BlockSpec((tm, tk), lambda i, kk: (i, kk)),
                  pl.BlockSpec((tk, n), lambda i, kk: (kk, 0)),
                  pl.BlockSpec((tm, n), lambda i, kk: (i, 0))],
        out_specs=pl.BlockSpec((tm, n), lambda i, kk: (i, 0)),
        out_shape=jax.ShapeDtypeStruct((t, n), F32),
        compiler_params=_params("parallel", "arbitrary"),
        name=name,
    )(a, w, resid)


def _gdn_kernel(qkvz_ref, ab_ref, convw_ref, alog_ref, dtb_ref, normw_ref, o_ref,
                xs_ref, act_ref, kt_ref, g_ref, gt_ref, beta_ref, state_ref):
    nbatch, tb = qkvz_ref.shape[0], qkvz_ref.shape[1]
    c_len, d, nh, w = GDN_CHUNK, HEAD_DIM, GDN_HEADS, GDN_WIDTH
    n_chunks = tb // c_len

    @pl.when(pl.program_id(0) == 0)
    def _():
        xs_ref[:, 0:CONV_TAIL_ROWS, :] = jnp.zeros((nbatch, CONV_TAIL_ROWS, 3 * w), F32)
        state_ref[...] = jnp.zeros(state_ref.shape, F32)

    ri = lax.broadcasted_iota(jnp.int32, (tb, tb), 0)
    ci = lax.broadcasted_iota(jnp.int32, (tb, tb), 1)
    tri = ((ci <= ri) & (ri // c_len == ci // c_len)).astype(F32)

    for b in range(nbatch):
        xs_ref[b, CONV_TAIL_ROWS:CONV_TAIL_ROWS + tb, :] = qkvz_ref[b, :, 0:3 * w].astype(F32)
        for cg in range(3 * nh):
            cols = slice(cg * d, (cg + 1) * d)
            acc = (xs_ref[b, CONV_TAIL_ROWS:CONV_TAIL_ROWS + tb, cols]
                   * convw_ref[GDN_CONV - 1:GDN_CONV, cols])
            for s in range(1, GDN_CONV):
                acc = acc + (xs_ref[b, CONV_TAIL_ROWS - s:CONV_TAIL_ROWS - s + tb, cols]
                             * convw_ref[GDN_CONV - 1 - s:GDN_CONV - s, cols])
            y = _silu(acc)
            if cg < 2 * nh:
                y = y * lax.rsqrt(jnp.sum(y * y, axis=-1, keepdims=True) + EPS)
                if cg < nh:
                    y = y * (d ** -0.5)
            act_ref[b, :, cols] = y
            if nh <= cg < 2 * nh:
                for c in range(n_chunks):
                    kt_ref[b, c, cg - nh] = y[c * c_len:(c + 1) * c_len, :].T
        xs_ref[b, 0:CONV_TAIL_ROWS, :] = xs_ref[b, tb:tb + CONV_TAIL_ROWS, :]

        ab = ab_ref[b]
        beta_ref[b] = _sigmoid(ab)
        z = ab + dtb_ref[...]
        softplus = jnp.maximum(z, 0.0) + jnp.log1p(jnp.exp(-jnp.abs(z)))
        g = -jnp.exp(alog_ref[...]) * softplus
        g_cum = jnp.dot(tri, g, precision=lax.Precision.HIGHEST, preferred_element_type=F32)
        g_ref[b] = g_cum
        for c in range(n_chunks):
            gt_ref[b, c] = g_cum[c * c_len:(c + 1) * c_len, :].T

    ii = lax.broadcasted_iota(jnp.int32, (c_len, c_len), 0)
    jj = lax.broadcasted_iota(jnp.int32, (c_len, c_len), 1)
    incl = ii >= jj
    strict = ii > jj
    norm_w = normw_ref[...]
    chains = [(b, h) for b in range(nbatch) for h in range(nh)]

    def mm(a, b):
        return jnp.dot(a.astype(BF16), b.astype(BF16), preferred_element_type=F32)

    def chunk_body(c, carry):
        rows = slice(c * c_len, (c + 1) * c_len)
        g_all = [g_ref[b, rows, :] for b in range(nbatch)]
        beta_all = [beta_ref[b, rows, :] for b in range(nbatch)]
        gt_all = [gt_ref[b, c] for b in range(nbatch)]

        st = []
        for b, h in chains:
            q = act_ref[b, rows, h * d:(h + 1) * d]
            k = act_ref[b, rows, w + h * d:w + (h + 1) * d]
            v = act_ref[b, rows, 2 * w + h * d:2 * w + (h + 1) * d]
            kt = kt_ref[b, c, h]
            g_col = g_all[b][:, h:h + 1]
            b_col = beta_all[b][:, nh + h:nh + h + 1]
            g_row = gt_all[b][h:h + 1, :]
            g_last = g_row[:, c_len - 1:c_len]
            decay = jnp.where(incl, jnp.exp(jnp.where(incl, g_col - g_row, 0.0)), 0.0)
            e_g = jnp.exp(g_col)
            kb = k * b_col
            st.append(dict(q=q, kt=kt, g_row=g_row, g_last=g_last, decay=decay, kb=kb,
                           qe=q * e_g, rhs=jnp.concatenate([v * b_col, kb * e_g], axis=1)))
        for s in st:
            s1 = mm(jnp.concatenate([s["kb"], s["q"]], axis=0), s["kt"])
            s["a"] = jnp.where(strict, s1[:c_len] * s["decay"], 0.0)
            s["a_qk"] = jnp.where(incl, s1[c_len:] * s["decay"], 0.0)
        for s in st:
            s["x"] = s["rhs"] - mm(s["a"], s["rhs"])
            s["m"] = mm(s["a"], s["a"])
        for it in range(5):
            for s in st:
                s["x"] = s["x"] + mm(s["m"], s["x"])
                if it < 4:
                    s["m"] = mm(s["m"], s["m"])
        for (b, h), s in zip(chains, st):
            s["state"] = state_ref[b, h]
            s["ws"] = mm(jnp.concatenate([s["x"][:, d:], s["qe"]], axis=0), s["state"])
        for (b, h), s in zip(chains, st):
            v_new = s["x"][:, :d] - s["ws"][:c_len]
            o = s["ws"][c_len:] + mm(s["a_qk"], v_new)
            kdt = s["kt"] * jnp.exp(s["g_last"] - s["g_row"])
            state_ref[b, h] = s["state"] * jnp.exp(s["g_last"]) + mm(kdt, v_new)
            o = o * lax.rsqrt(jnp.mean(o * o, axis=-1, keepdims=True) + EPS)
            zh = qkvz_ref[b, rows, 3 * w + h * d:3 * w + (h + 1) * d].astype(F32)
            o_ref[b, rows, h * d:(h + 1) * d] = (o * norm_w * _silu(zh)).astype(o_ref.dtype)
        return carry

    for c in range(n_chunks):
        chunk_body(c, 0)


def _gdn(main, ab, conv_w, a_log, dt_bias, norm_w, batch, seq):
    tb = GDN_BLOCK
    w = GDN_WIDTH
    n_chunks = tb // GDN_CHUNK
    blk = lambda s: (0, s, 0)
    const = lambda s: (0, 0)
    out = pl.pallas_call(
        _gdn_kernel,
        grid=(seq // tb,),
        in_specs=[pl.BlockSpec((batch, tb, 4 * w), blk),
                  pl.BlockSpec((batch, tb, LANES), blk),
                  pl.BlockSpec((GDN_CONV, 3 * w), const),
                  pl.BlockSpec((1, LANES), const),
                  pl.BlockSpec((1, LANES), const),
                  pl.BlockSpec((1, HEAD_DIM), const)],
        out_specs=pl.BlockSpec((batch, tb, w), blk),
        out_shape=jax.ShapeDtypeStruct((batch, seq, w), BF16),
        scratch_shapes=[pltpu.VMEM((batch, tb + CONV_TAIL_ROWS, 3 * w), F32),
                        pltpu.VMEM((batch, tb, 3 * w), F32),
                        pltpu.VMEM((batch, n_chunks, GDN_HEADS, HEAD_DIM, GDN_CHUNK), F32),
                        pltpu.VMEM((batch, tb, LANES), F32),
                        pltpu.VMEM((batch, n_chunks, LANES, GDN_CHUNK), F32),
                        pltpu.VMEM((batch, tb, LANES), F32),
                        pltpu.VMEM((batch, GDN_HEADS, HEAD_DIM, HEAD_DIM), F32)],
        compiler_params=_params("arbitrary"),
        name="gdn",
    )(main.reshape(batch, seq, main.shape[1]), ab.reshape(batch, seq, LANES), conv_w, a_log, dt_bias, norm_w)
    return out.reshape(batch * seq, w)


def _swa_kernel(x_ref, o_ref,
                q_s, k_s0, k_s1, k_s2, v_s0, v_s1, v_s2, og_s, lg_s):
    tq, d, blk = x_ref.shape[0], HEAD_DIM, SWA_BLOCK
    first = pl.program_id(2) == 0
    k_ss, v_ss = (k_s0, k_s1, k_s2), (v_s0, v_s1, v_s2)
    qi = lax.broadcasted_iota(jnp.int32, (blk, 2 * blk), 0)
    cj = lax.broadcasted_iota(jnp.int32, (blk, 2 * blk), 1)
    band = (cj >= qi) & (cj <= qi + blk)
    band_first = band & (cj >= jnp.where(first, blk, 0))
    ones_v = jnp.ones((2 * blk, d), BF16)

    for g, (window, dl) in enumerate(SWA_GROUPS):
        tail = blk * dl
        k_s, v_s = k_ss[g], v_ss[g]

        @pl.when(first)
        def _(k_s=k_s, v_s=v_s, tail=tail):
            k_s[0:tail, :] = jnp.zeros((tail, d), F32)
            v_s[0:tail, :] = jnp.zeros((tail, d), F32)

        base = 3 * g * d
        q_s[...] = x_ref[:, base:base + d].astype(F32)
        k_s[tail:tail + tq, :] = x_ref[:, base + d:base + 2 * d].astype(F32)
        v_s[tail:tail + tq, :] = x_ref[:, base + 2 * d:base + 3 * d].astype(F32)
        for r in range(dl):
            for i in range(tq // tail):
                start = tail * i + r
                if dl == 1:
                    qrows, krows = pl.ds(start, blk), pl.ds(start, 2 * blk)
                else:
                    qrows, krows = pl.ds(start, blk, stride=dl), pl.ds(start, 2 * blk, stride=dl)
                q = q_s[qrows, :].astype(BF16)
                k = k_s[krows, :].astype(BF16)
                v = jnp.concatenate([v_s[krows, :].astype(BF16), ones_v], axis=1)
                s = lax.dot_general(q, k, (((1,), (1,)), ((), ())), preferred_element_type=F32)
                s = jnp.where(band_first if i == 0 else band, s, NEG_INF)
                mx = jnp.max(s, axis=-1, keepdims=True)
                p = jnp.exp(s - mx)
                pv = jnp.dot(p.astype(BF16), v, preferred_element_type=F32)
                den = pv[:, d:]
                og_s[g, qrows, :] = pv[:, :d] / den
                lg_s[g, qrows, :] = mx + jnp.log(den)
        k_s[0:tail, :] = k_s[tq:tq + tail, :]
        v_s[0:tail, :] = v_s[tq:tq + tail, :]

    l0, l1, l2 = lg_s[0], lg_s[1], lg_s[2]
    mx = jnp.maximum(jnp.maximum(l0, l1), l2)
    e0, e1, e2 = jnp.exp(l0 - mx), jnp.exp(l1 - mx), jnp.exp(l2 - mx)
    o_ref[...] = ((og_s[0] * e0 + og_s[1] * e1 + og_s[2] * e2) / (e0 + e1 + e2)).astype(o_ref.dtype)


def _swa(xs, batch, seq):
    t = xs.shape[0]
    d = HEAD_DIM
    tq = SWA_TOKENS
    nb = seq // tq
    slot = lambda b, j, n: (b * nb + n, j)
    dils = [dl for _, dl in SWA_GROUPS]
    return pl.pallas_call(
        _swa_kernel,
        grid=(batch, SWA_HEADS_PER_GROUP, nb),
        in_specs=[pl.BlockSpec((tq, 3 * len(SWA_GROUPS) * d), slot)],
        out_specs=pl.BlockSpec((tq, d), slot),
        out_shape=jax.ShapeDtypeStruct((t, SWA_GROUP_WIDTH), BF16),
        scratch_shapes=[pltpu.VMEM((tq, d), F32)]
        + [pltpu.VMEM((SWA_BLOCK * dl + tq, d), F32) for dl in dils] * 2
        + [pltpu.VMEM((len(SWA_GROUPS), tq, d), F32)] * 2,
        compiler_params=_params("parallel", "parallel", "arbitrary"),
        name="swa",
    )(xs)


def _mix_kernel(oa_ref, ob_ref, wg_ref, ws_ref, ga_ref, gb_ref, out_ref):
    ya = jnp.dot(oa_ref[...], wg_ref[...], preferred_element_type=F32)
    yb = jnp.dot(ob_ref[...], ws_ref[...], preferred_element_type=F32)
    mixed = _sigmoid(ga_ref[...].astype(F32)) * ya + _sigmoid(gb_ref[...].astype(F32)) * yb
    out_ref[...] = mixed.astype(out_ref.dtype)


def _mix(oa, ob, wg, ws, main, gate_block, tm):
    t = oa.shape[0]
    n = wg.shape[1]
    row = lambda i: (i, 0)
    const = lambda i: (0, 0)
    return pl.pallas_call(
        _mix_kernel,
        grid=(t // tm,),
        in_specs=[pl.BlockSpec((tm, oa.shape[1]), row), pl.BlockSpec((tm, ob.shape[1]), row),
                  pl.BlockSpec(wg.shape, const), pl.BlockSpec(ws.shape, const),
                  pl.BlockSpec((tm, n), lambda i: (i, gate_block)),
                  pl.BlockSpec((tm, n), lambda i: (i, gate_block + 1))],
        out_specs=pl.BlockSpec((tm, n), row),
        out_shape=jax.ShapeDtypeStruct((t, n), BF16),
        compiler_params=_params("parallel"),
        name="mix",
    )(oa, ob, wg, ws, main, main)


def _layer(x2, positions, batch, seq, ln1_w, w_in, conv_w, a_log, dt_bias, gdn_norm_w, w_gdn_proj,
           q_norm_w, k_norm_w, w_swa_proj, w_out, ln2_w, w_ff1, w_ff2):
    d_model = x2.shape[1]
    gw3 = 3 * GDN_WIDTH
    o_a, o_b = gw3 + GDN_WIDTH, gw3 + GDN_WIDTH + GDN_HEADS
    o_swa = o_b + GDN_HEADS
    o_ga = o_swa + 3 * SWA_WIDTH
    assert seq % SWA_TOKENS == 0 and seq % GDN_BLOCK == 0
    assert all(window // dl == SWA_BLOCK and SWA_TOKENS % (dl * SWA_BLOCK) == 0 for window, dl in SWA_GROUPS)
    swa_cols = []
    for j in range(SWA_HEADS_PER_GROUP):
        for g in range(len(SWA_GROUPS)):
            for part in range(3):
                lo = o_swa + part * SWA_WIDTH + (g * SWA_HEADS_PER_GROUP + j) * HEAD_DIM
                swa_cols.append(w_in[:, lo:lo + HEAD_DIM])
    w_main = jnp.concatenate([w_in[:, :o_a], w_in[:, o_ga:]], axis=1).astype(BF16)
    w_swa = jnp.concatenate(swa_cols, axis=1).astype(BF16)
    w_ab = jnp.pad(w_in[:, o_a:o_swa], ((0, 0), (0, LANES - 2 * GDN_HEADS))).astype(BF16)
    ln1 = ln1_w.reshape(1, -1)
    main, ab, h1 = _norm_matmul(x2, ln1, w_main, tm=1024, tn=2048, name="in_proj_main", w_ab=w_ab)

    pad8 = lambda v: jnp.pad(v.astype(F32), (0, LANES - v.shape[0])).reshape(1, LANES)
    o_a_branch = _gdn(main, ab, conv_w.astype(F32), pad8(a_log), pad8(dt_bias),
                      gdn_norm_w.reshape(1, -1).astype(F32), batch, seq)

    half = HEAD_DIM // 2
    inv_freq = ROPE_THETA ** (-jnp.arange(half, dtype=F32) / half)
    cos_t, sin_t = _rope_tables(positions.reshape(-1, 1), jnp.concatenate([inv_freq, inv_freq]).reshape(1, HEAD_DIM))
    swa_in = _swa_proj(h1, w_swa, cos_t, sin_t, q_norm_w.reshape(1, -1).astype(F32),
                       k_norm_w.reshape(1, -1).astype(F32), tm=1024, tn=SWA_PROJ_TILE_N)
    o_b_branch = _swa(swa_in, batch, seq)

    mixed = _mix(o_a_branch, o_b_branch, w_gdn_proj.astype(BF16), w_swa_proj.astype(BF16), main,
                 gate_block=4 * GDN_WIDTH // d_model, tm=512)
    x2 = _residual_matmul(mixed, w_out.astype(BF16), x2, tm=512, tk=d_model, name="out_proj")
    u2 = _norm_matmul(x2, ln2_w.reshape(1, -1), w_ff1.astype(BF16), tm=1024, tn=2048, name="ff1", square_relu=True)
    return _residual_matmul(u2, w_ff2.astype(BF16), x2, tm=1024, tk=1024, name="ff2")


def kernel(x, positions, ln1_w, w_in, gdn_conv_w, gdn_a_log, gdn_dt_bias, gdn_norm_w, w_gdn_proj,
           swa_q_norm_w, swa_k_norm_w, w_swa_proj, w_out, ln2_w, w_ff1, w_ff2):
    batch, seq, d_model = x.shape
    x2 = x.reshape(batch * seq, d_model)
    for l in range(ln1_w.shape[0]):
        x2 = _layer(x2, positions, batch, seq, ln1_w[l], w_in[l], gdn_conv_w[l], gdn_a_log[l], gdn_dt_bias[l],
                    gdn_norm_w[l], w_gdn_proj[l], swa_q_norm_w[l], swa_k_norm_w[l], w_swa_proj[l], w_out[l],
                    ln2_w[l], w_ff1[l], w_ff2[l])
    return x2.reshape(batch, seq, d_model)
```

```python
import functools

import jax
import jax.numpy as jnp
from jax import lax
from jax.experimental import pallas as pl
from jax.experimental.pallas import tpu as pltpu

F32 = jnp.float32
BF16 = jnp.bfloat16

HEAD_DIM = 128
EPS = 1e-6
GDN_HEADS = 8
GDN_WIDTH = GDN_HEADS * HEAD_DIM
GDN_CONV = 4
GDN_CHUNK = 64
SWA_GROUPS = ((128, 1), (512, 4), (2048, 16))
SWA_HEADS_PER_GROUP = 4
SWA_GROUP_WIDTH = SWA_HEADS_PER_GROUP * HEAD_DIM
SWA_WIDTH = len(SWA_GROUPS) * SWA_GROUP_WIDTH
SWA_BLOCK = 128
ROPE_THETA = 10000.0
NEG_INF = -1e30

VMEM_LIMIT_BYTES = 56 * 1024 * 1024
LANES = 128
SUBLANES = 8
CONV_TAIL_ROWS = SUBLANES
NORM_ROWS = 256
GDN_BLOCK = 256
SWA_TOKENS = 2048
SWA_PROJ_TILE_N = 2304
MAIN_TILES = 4


def _params(*sem):
    return pltpu.CompilerParams(dimension_semantics=sem, vmem_limit_bytes=VMEM_LIMIT_BYTES)


def _sigmoid(x):
    return 0.5 * jnp.tanh(0.5 * x) + 0.5


def _silu(x):
    h = 0.5 * x
    return h * jnp.tanh(h) + h


def _rope_table_kernel(pos_ref, invf_ref, cos_ref, sin_ref):
    ang = pos_ref[...].astype(F32) * invf_ref[...]
    cos_ref[...] = jnp.cos(ang)
    lane = lax.broadcasted_iota(jnp.int32, ang.shape, 1)
    s = jnp.sin(ang)
    sin_ref[...] = jnp.where(lane < HEAD_DIM // 2, -s, s)


def _rope_tables(pos_col, inv_freq2):
    t = pos_col.shape[0]
    tm = min(t, 2048)
    return pl.pallas_call(
        _rope_table_kernel,
        grid=(t // tm,),
        in_specs=[pl.BlockSpec((tm, 1), lambda i: (i, 0)),
                  pl.BlockSpec((1, HEAD_DIM), lambda i: (0, 0))],
        out_specs=[pl.BlockSpec((tm, HEAD_DIM), lambda i: (i, 0)),
                   pl.BlockSpec((tm, HEAD_DIM), lambda i: (i, 0))],
        out_shape=[jax.ShapeDtypeStruct((t, HEAD_DIM), F32)] * 2,
        compiler_params=_params("parallel"),
        name="rope_table",
    )(pos_col, inv_freq2)


def _rmsnorm_to(h_ref, x_ref, lnw_ref):
    tm = x_ref.shape[0]
    w = lnw_ref[...]

    def body(c, carry):
        rows = pl.ds(pl.multiple_of(c * NORM_ROWS, NORM_ROWS), NORM_ROWS)
        x = x_ref[rows, :]
        y = x * lax.rsqrt(jnp.mean(x * x, axis=-1, keepdims=True) + EPS)
        h_ref[rows, :] = (y * w).astype(BF16)
        return carry

    lax.fori_loop(0, tm // NORM_ROWS, body, 0)


def _norm_matmul_kernel(x_ref, lnw_ref, w_ref, *rest, with_ab, square_relu):
    if with_ab:
        wab_ref, o_ref, ab_ref, h_ref = rest
    else:
        o_ref, h_ref = rest

    @pl.when(pl.program_id(1) == 0)
    def _():
        _rmsnorm_to(h_ref, x_ref, lnw_ref)
        if with_ab:
            ab_ref[...] = jnp.dot(h_ref[...], wab_ref[...], preferred_element_type=F32)

    acc = jnp.dot(h_ref[...], w_ref[...], preferred_element_type=F32)
    if square_relu:
        acc = jnp.maximum(acc, 0.0)
        acc = acc * acc
    o_ref[...] = acc.astype(o_ref.dtype)


def _norm_matmul(x2, ln_w, w, tm, tn, name, w_ab=None, square_relu=False):
    t, k = x2.shape
    n = w.shape[1]
    in_specs = [pl.BlockSpec((tm, k), lambda i, j: (i, 0)),
                pl.BlockSpec((1, k), lambda i, j: (0, 0)),
                pl.BlockSpec((k, tn), lambda i, j: (0, j))]
    out_specs = [pl.BlockSpec((tm, tn), lambda i, j: (i, j))]
    out_shape = [jax.ShapeDtypeStruct((t, n), BF16)]
    args = [x2, ln_w, w]
    if w_ab is not None:
        in_specs.append(pl.BlockSpec((k, LANES), lambda i, j: (0, 0)))
        out_specs += [pl.BlockSpec((tm, LANES), lambda i, j: (i, 0)), pl.BlockSpec((tm, k), lambda i, j: (i, 0))]
        out_shape += [jax.ShapeDtypeStruct((t, LANES), F32), jax.ShapeDtypeStruct((t, k), BF16)]
        args.append(w_ab)
    out = pl.pallas_call(
        functools.partial(_norm_matmul_kernel, with_ab=w_ab is not None, square_relu=square_relu),
        grid=(t // tm, n // tn),
        in_specs=in_specs, out_specs=out_specs, out_shape=out_shape,
        scratch_shapes=[] if w_ab is not None else [pltpu.VMEM((tm, k), BF16)],
        compiler_params=_params("parallel", "arbitrary"),
        name=name,
    )(*args)
    return out if w_ab is not None else out[0]


def _main_proj_kernel(x_ref, lnw_ref, w_ref, wab_ref, convw_ref, act_ref, pass_ref, ab_ref, h_ref, acc_ref, tail_ref,
                      *, nj, conv_cols, tiles_per_seq, n_row_tiles):
    d, tm = HEAD_DIM, x_ref.shape[0]
    top = CONV_TAIL_ROWS
    step = pl.program_id(0)
    row_tile, j = step // nj, step % nj

    @pl.when(step == 0)
    def _():
        acc_ref[...] = jnp.zeros(acc_ref.shape, F32)
        tail_ref[...] = jnp.zeros(tail_ref.shape, F32)

    @pl.when((j == 0) & (row_tile < n_row_tiles))
    def _():
        _rmsnorm_to(h_ref, x_ref, lnw_ref)
        ab_ref[...] = jnp.dot(h_ref[...], wab_ref[...], preferred_element_type=F32)

    seq_start = (row_tile - 1) % tiles_per_seq == 0

    def finish_and_multiply(jt, multiply):
        c0 = jt * conv_cols
        acc_ref[0:top, c0:c0 + conv_cols] = jnp.where(seq_start, 0.0, tail_ref[:, c0:c0 + conv_cols])
        new_tail = acc_ref[tm:tm + top, c0:c0 + conv_cols]
        for r0 in range(0, tm, NORM_ROWS):
            for cg in range(conv_cols // d):
                cols = slice(c0 + cg * d, c0 + (cg + 1) * d)
                head = (c0 + cg * d) // d
                y = acc_ref[top + r0:top + r0 + NORM_ROWS, cols] * convw_ref[GDN_CONV - 1:GDN_CONV, cols]
                for s in range(1, GDN_CONV):
                    y = y + (acc_ref[top + r0 - s:top + r0 - s + NORM_ROWS, cols]
                             * convw_ref[GDN_CONV - 1 - s:GDN_CONV - s, cols])
                y = _silu(y)
                if head < 2 * GDN_HEADS:
                    y = y * lax.rsqrt(jnp.sum(y * y, axis=-1, keepdims=True) + EPS)
                    if head < GDN_HEADS:
                        y = y * (d ** -0.5)
                act_ref[r0:r0 + NORM_ROWS, cg * d:(cg + 1) * d] = y.astype(act_ref.dtype)
        tail_ref[:, c0:c0 + conv_cols] = new_tail
        if multiply:
            prod = jnp.dot(h_ref[...], w_ref[...], preferred_element_type=F32)
            acc_ref[top:top + tm, c0:c0 + conv_cols] = prod[:, :conv_cols]
            pass_ref[...] = prod[:, conv_cols:].astype(pass_ref.dtype)

    for jt in range(nj):
        pl.when((j == jt) & (row_tile < n_row_tiles))(functools.partial(finish_and_multiply, jt, True))
        pl.when((j == jt) & (row_tile == n_row_tiles))(functools.partial(finish_and_multiply, jt, False))


def _main_proj(x2, ln_w, w, w_ab, conv_w, tm, seq, nj):
    t, k = x2.shape
    n = w.shape[1]
    conv_n = 3 * GDN_WIDTH
    tn = n // nj
    conv_cols = conv_n // nj
    pass_cols = tn - conv_cols
    n_row_tiles = t // tm
    assert conv_cols % HEAD_DIM == 0 and pass_cols % LANES == 0 and seq % tm == 0 and tm % NORM_ROWS == 0
    last = n_row_tiles - 1
    cur_r = lambda s: jnp.minimum(s // nj, last)
    prev_r = lambda s: jnp.clip(s // nj - 1, 0, last)
    const = lambda s: (0, 0)
    return pl.pallas_call(
        functools.partial(_main_proj_kernel, nj=nj, conv_cols=conv_cols, tiles_per_seq=seq // tm,
                          n_row_tiles=n_row_tiles),
        grid=((n_row_tiles + 1) * nj,),
        in_specs=[pl.BlockSpec((tm, k), lambda s: (cur_r(s), 0)),
                  pl.BlockSpec((1, k), const),
                  pl.BlockSpec((k, tn), lambda s: (0, s % nj)),
                  pl.BlockSpec((k, LANES), const),
                  pl.BlockSpec(conv_w.shape, const)],
        out_specs=[pl.BlockSpec((tm, conv_cols), lambda s: (prev_r(s), jnp.where(s < nj, 0, s % nj))),
                   pl.BlockSpec((tm, pass_cols), lambda s: (cur_r(s), jnp.where(s >= n_row_tiles * nj, nj - 1, s % nj))),
                   pl.BlockSpec((tm, LANES), lambda s: (cur_r(s), 0)),
                   pl.BlockSpec((tm, k), lambda s: (cur_r(s), 0))],
        out_shape=[jax.ShapeDtypeStruct((t, conv_n), BF16), jax.ShapeDtypeStruct((t, n - conv_n), BF16),
                   jax.ShapeDtypeStruct((t, LANES), F32), jax.ShapeDtypeStruct((t, k), BF16)],
        scratch_shapes=[pltpu.VMEM((tm + CONV_TAIL_ROWS, conv_n), F32),
                        pltpu.VMEM((CONV_TAIL_ROWS, conv_n), F32)],
        compiler_params=_params("arbitrary"),
        name="in_proj_main",
    )(x2, ln_w, w, w_ab, conv_w)


def _swa_proj_kernel(h_ref, w_ref, cos_ref, sin_ref, qnw_ref, knw_ref, o_ref, acc_ref):
    d = HEAD_DIM

    @pl.when(pl.program_id(0) == 0)
    def _():
        acc_ref[...] = jnp.zeros(acc_ref.shape, F32)

    cos, sin = cos_ref[...], sin_ref[...]
    norm_w = (qnw_ref[...] * (d ** -0.5), knw_ref[...])
    for g in range(acc_ref.shape[1] // d):
        y = acc_ref[:, g * d:(g + 1) * d]
        if g % 3 < 2:
            y = y * lax.rsqrt(jnp.mean(y * y, axis=-1, keepdims=True) + EPS) * norm_w[g % 3]
            y = y * cos + pltpu.roll(y, d // 2, 1) * sin
        o_ref[:, g * d:(g + 1) * d] = y.astype(o_ref.dtype)
    acc_ref[...] = jnp.dot(h_ref[...], w_ref[...], preferred_element_type=F32)


def _swa_proj(h, w, cos_t, sin_t, qnw, knw, tm, tn):
    t, k = h.shape
    n = w.shape[1]
    nj = n // tn
    n_tiles = (t // tm) * nj
    cur = lambda s: jnp.minimum(s, n_tiles - 1)
    prev = lambda s: jnp.maximum(s - 1, 0)
    const = lambda s: (0, 0)
    return pl.pallas_call(
        _swa_proj_kernel,
        grid=(n_tiles + 1,),
        in_specs=[pl.BlockSpec((tm, k), lambda s: (cur(s) // nj, 0)),
                  pl.BlockSpec((k, tn), lambda s: (0, cur(s) % nj)),
                  pl.BlockSpec((tm, HEAD_DIM), lambda s: (prev(s) // nj, 0)),
                  pl.BlockSpec((tm, HEAD_DIM), lambda s: (prev(s) // nj, 0)),
                  pl.BlockSpec((1, HEAD_DIM), const), pl.BlockSpec((1, HEAD_DIM), const)],
        out_specs=pl.BlockSpec((tm, tn), lambda s: (prev(s) // nj, prev(s) % nj)),
        out_shape=jax.ShapeDtypeStruct((t, n), BF16),
        scratch_shapes=[pltpu.VMEM((tm, tn), F32)],
        compiler_params=_params("arbitrary"),
        name="in_proj_swa",
    )(h, w, cos_t, sin_t, qnw, knw)


def _residual_matmul_kernel(a_ref, w_ref, r_ref, o_ref):
    @pl.when(pl.program_id(1) == 0)
    def _():
        o_ref[...] = r_ref[...]

    o_ref[...] += jnp.dot(a_ref[...], w_ref[...], preferred_element_type=F32)


def _residual_matmul(a, w, resid, tm, tk, name):
    t, k = a.shape
    n = w.shape[1]
    return pl.pallas_call(
        _residual_matmul_kernel,
        grid=(t // tm, k // tk),
        in_specs=[pl.BlockSpec((tm, tk), lambda i, kk: (i, kk)),
                  pl.BlockSpec((tk, n), lambda i, kk: (kk, 0)),
                  pl.BlockSpec((tm, n), lambda i, kk: (i, 0))],
        out_specs=pl.BlockSpec((tm, n), lambda i, kk: (i, 0)),
        out_shape=jax.ShapeDtypeStruct((t, n), F32),
        compiler_params=_params("parallel", "arbitrary"),
        name=name,
    )(a, w, resid)


def _gdn_kernel(qkv_ref, z_ref, ab_ref, alog_ref, dtb_ref, normw_ref, o_ref,
                act_ref, kt_ref, g_ref, gt_ref, beta_ref, state_ref):
    nbatch, tb = qkv_ref.shape[0], qkv_ref.shape[1]
    c_len, d, nh, w = GDN_CHUNK, HEAD_DIM, GDN_HEADS, GDN_WIDTH
    n_chunks = tb // c_len

    @pl.when(pl.program_id(0) == 0)
    def _():
        state_ref[...] = jnp.zeros(state_ref.shape, F32)

    ri = lax.broadcasted_iota(jnp.int32, (tb, tb), 0)
    ci = lax.broadcasted_iota(jnp.int32, (tb, tb), 1)
    tri = ((ci <= ri) & (ri // c_len == ci // c_len)).astype(F32)

    for b in range(nbatch):
        for cg in range(3 * nh):
            cols = slice(cg * d, (cg + 1) * d)
            y = qkv_ref[b, :, cols].astype(F32)
            act_ref[b, :, cols] = y
            if nh <= cg < 2 * nh:
                for c in range(n_chunks):
                    kt_ref[b, c, cg - nh] = y[c * c_len:(c + 1) * c_len, :].T

        ab = ab_ref[b]
        beta_ref[b] = _sigmoid(ab)
        z = ab + dtb_ref[...]
        softplus = jnp.maximum(z, 0.0) + jnp.log1p(jnp.exp(-jnp.abs(z)))
        g = -jnp.exp(alog_ref[...]) * softplus
        g_cum = jnp.dot(tri, g, precision=lax.Precision.HIGHEST, preferred_element_type=F32)
        g_ref[b] = g_cum
        for c in range(n_chunks):
            gt_ref[b, c] = g_cum[c * c_len:(c + 1) * c_len, :].T

    ii = lax.broadcasted_iota(jnp.int32, (c_len, c_len), 0)
    jj = lax.broadcasted_iota(jnp.int32, (c_len, c_len), 1)
    incl = ii >= jj
    strict = ii > jj
    norm_w = normw_ref[...]
    chains = [(b, h) for b in range(nbatch) for h in range(nh)]

    def mm(a, b):
        return jnp.dot(a.astype(BF16), b.astype(BF16), preferred_element_type=F32)

    def chunk_body(c, carry):
        rows = slice(c * c_len, (c + 1) * c_len)
        g_all = [g_ref[b, rows, :] for b in range(nbatch)]
        beta_all = [beta_ref[b, rows, :] for b in range(nbatch)]
        gt_all = [gt_ref[b, c] for b in range(nbatch)]

        st = []
        for b, h in chains:
            q = act_ref[b, rows, h * d:(h + 1) * d]
            k = act_ref[b, rows, w + h * d:w + (h + 1) * d]
            v = act_ref[b, rows, 2 * w + h * d:2 * w + (h + 1) * d]
            kt = kt_ref[b, c, h]
            g_col = g_all[b][:, h:h + 1]
            b_col = beta_all[b][:, nh + h:nh + h + 1]
            g_row = gt_all[b][h:h + 1, :]
            g_last = g_row[:, c_len - 1:c_len]
            decay = jnp.where(incl, jnp.exp(jnp.where(incl, g_col - g_row, 0.0)), 0.0)
            e_g = jnp.exp(g_col)
            kb = k * b_col
            st.append(dict(q=q, kt=kt, g_row=g_row, g_last=g_last, decay=decay, kb=kb,
                           qe=q * e_g, rhs=jnp.concatenate([v * b_col, kb * e_g], axis=1)))
        for s in st:
            s1 = mm(jnp.concatenate([s["kb"], s["q"]], axis=0), s["kt"])
            s["a"] = jnp.where(strict, s1[:c_len] * s["decay"], 0.0)
            s["a_qk"] = jnp.where(incl, s1[c_len:] * s["decay"], 0.0)
        for s in st:
            s["x"] = s["rhs"] - mm(s["a"], s["rhs"])
            s["m"] = mm(s["a"], s["a"])
        for it in range(5):
            for s in st:
                s["x"] = s["x"] + mm(s["m"], s["x"])
                if it < 4:
                    s["m"] = mm(s["m"], s["m"])
        for (b, h), s in zip(chains, st):
            s["state"] = state_ref[b, h]
            s["ws"] = mm(jnp.concatenate([s["x"][:, d:], s["qe"]], axis=0), s["state"])
        for (b, h), s in zip(chains, st):
            v_new = s["x"][:, :d] - s["ws"][:c_len]
            o = s["ws"][c_len:] + mm(s["a_qk"], v_new)
            kdt = s["kt"] * jnp.exp(s["g_last"] - s["g_row"])
            state_ref[b, h] = s["state"] * jnp.exp(s["g_last"]) + mm(kdt, v_new)
            o = o * lax.rsqrt(jnp.mean(o * o, axis=-1, keepdims=True) + EPS)
            zh = z_ref[b, rows, h * d:(h + 1) * d].astype(F32)
            o_ref[b, rows, h * d:(h + 1) * d] = (o * norm_w * _silu(zh)).astype(o_ref.dtype)
        return carry

    for c in range(n_chunks):
        chunk_body(c, 0)


def _gdn(act, passed, z_block, ab, a_log, dt_bias, norm_w, batch, seq):
    tb = GDN_BLOCK
    w = GDN_WIDTH
    n_chunks = tb // GDN_CHUNK
    blk = lambda s: (0, s, 0)
    const = lambda s: (0, 0)
    out = pl.pallas_call(
        _gdn_kernel,
        grid=(seq // tb,),
        in_specs=[pl.BlockSpec((batch, tb, 3 * w), blk),
                  pl.BlockSpec((batch, tb, w), lambda s: (0, s, z_block)),
                  pl.BlockSpec((batch, tb, LANES), blk),
                  pl.BlockSpec((1, LANES), const),
                  pl.BlockSpec((1, LANES), const),
                  pl.BlockSpec((1, HEAD_DIM), const)],
        out_specs=pl.BlockSpec((batch, tb, w), blk),
        out_shape=jax.ShapeDtypeStruct((batch, seq, w), BF16),
        scratch_shapes=[pltpu.VMEM((batch, tb, 3 * w), F32),
                        pltpu.VMEM((batch, n_chunks, GDN_HEADS, HEAD_DIM, GDN_CHUNK), F32),
                        pltpu.VMEM((batch, tb, LANES), F32),
                        pltpu.VMEM((batch, n_chunks, LANES, GDN_CHUNK), F32),
                        pltpu.VMEM((batch, tb, LANES), F32),
                        pltpu.VMEM((batch, GDN_HEADS, HEAD_DIM, HEAD_DIM), F32)],
        compiler_params=_params("arbitrary"),
        name="gdn",
    )(act.reshape(batch, seq, 3 * w), passed.reshape(batch, seq, passed.shape[1]), ab.reshape(batch, seq, LANES),
      a_log, dt_bias, norm_w)
    return out.reshape(batch * seq, w)


def _swa_kernel(x_ref, o_ref,
                q_s, k_s0, k_s1, k_s2, v_s0, v_s1, v_s2, og_s, lg_s):
    tq, d, blk = x_ref.shape[0], HEAD_DIM, SWA_BLOCK
    first = pl.program_id(2) == 0
    k_ss, v_ss = (k_s0, k_s1, k_s2), (v_s0, v_s1, v_s2)
    qi = lax.broadcasted_iota(jnp.int32, (blk, 2 * blk), 0)
    cj = lax.broadcasted_iota(jnp.int32, (blk, 2 * blk), 1)
    band = (cj >= qi) & (cj <= qi + blk)
    band_first = band & (cj >= jnp.where(first, blk, 0))
    ones_v = jnp.ones((2 * blk, d), BF16)

    for g, (window, dl) in enumerate(SWA_GROUPS):
        tail = blk * dl
        k_s, v_s = k_ss[g], v_ss[g]

        @pl.when(first)
        def _(k_s=k_s, v_s=v_s, tail=tail):
            k_s[0:tail, :] = jnp.zeros((tail, d), F32)
            v_s[0:tail, :] = jnp.zeros((tail, d), F32)

        base = 3 * g * d
        q_s[...] = x_ref[:, base:base + d].astype(F32)
        k_s[tail:tail + tq, :] = x_ref[:, base + d:base + 2 * d].astype(F32)
        v_s[tail:tail + tq, :] = x_ref[:, base + 2 * d:base + 3 * d].astype(F32)
        for r in range(dl):
            for i in range(tq // tail):
                start = tail * i + r
                if dl == 1:
                    qrows, krows = pl.ds(start, blk), pl.ds(start, 2 * blk)
                else:
                    qrows, krows = pl.ds(start, blk, stride=dl), pl.ds(start, 2 * blk, stride=dl)
                q = q_s[qrows, :].astype(BF16)
                k = k_s[krows, :].astype(BF16)
                v = jnp.concatenate([v_s[krows, :].astype(BF16), ones_v], axis=1)
                s = lax.dot_general(q, k, (((1,), (1,)), ((), ())), preferred_element_type=F32)
                s = jnp.where(band_first if i == 0 else band, s, NEG_INF)
                mx = jnp.max(s, axis=-1, keepdims=True)
                p = jnp.exp(s - mx)
                pv = jnp.dot(p.astype(BF16), v, preferred_element_type=F32)
                den = pv[:, d:]
                og_s[g, qrows, :] = pv[:, :d] / den
                lg_s[g, qrows, :] = mx + jnp.log(den)
        k_s[0:tail, :] = k_s[tq:tq + tail, :]
        v_s[0:tail, :] = v_s[tq:tq + tail, :]

    l0, l1, l2 = lg_s[0], lg_s[1], lg_s[2]
    mx = jnp.maximum(jnp.maximum(l0, l1), l2)
    e0, e1, e2 = jnp.exp(l0 - mx), jnp.exp(l1 - mx), jnp.exp(l2 - mx)
    o_ref[...] = ((og_s[0] * e0 + og_s[1] * e1 + og_s[2] * e2) / (e0 + e1 + e2)).astype(o_ref.dtype)


def _swa(xs, batch, seq):
    t = xs.shape[0]
    d = HEAD_DIM
    tq = SWA_TOKENS
    nb = seq // tq
    slot = lambda b, j, n: (b * nb + n, j)
    dils = [dl for _, dl in SWA_GROUPS]
    return pl.pallas_call(
        _swa_kernel,
        grid=(batch, SWA_HEADS_PER_GROUP, nb),
        in_specs=[pl.BlockSpec((tq, 3 * len(SWA_GROUPS) * d), slot)],
        out_specs=pl.BlockSpec((tq, d), slot),
        out_shape=jax.ShapeDtypeStruct((t, SWA_GROUP_WIDTH), BF16),
        scratch_shapes=[pltpu.VMEM((tq, d), F32)]
        + [pltpu.VMEM((SWA_BLOCK * dl + tq, d), F32) for dl in dils] * 2
        + [pltpu.VMEM((len(SWA_GROUPS), tq, d), F32)] * 2,
        compiler_params=_params("parallel", "parallel", "arbitrary"),
        name="swa",
    )(xs)


def _mix_kernel(oa_ref, ob_ref, wg_ref, ws_ref, ga_ref, gb_ref, out_ref):
    ya = jnp.dot(oa_ref[...], wg_ref[...], preferred_element_type=F32)
    yb = jnp.dot(ob_ref[...], ws_ref[...], preferred_element_type=F32)
    mixed = _sigmoid(ga_ref[...].astype(F32)) * ya + _sigmoid(gb_ref[...].astype(F32)) * yb
    out_ref[...] = mixed.astype(out_ref.dtype)


def _mix(oa, ob, wg, ws, main, gate_block, tm):
    t = oa.shape[0]
    n = wg.shape[1]
    row = lambda i: (i, 0)
    const = lambda i: (0, 0)
    return pl.pallas_call(
        _mix_kernel,
        grid=(t // tm,),
        in_specs=[pl.BlockSpec((tm, oa.shape[1]), row), pl.BlockSpec((tm, ob.shape[1]), row),
                  pl.BlockSpec(wg.shape, const), pl.BlockSpec(ws.shape, const),
                  pl.BlockSpec((tm, n), lambda i: (i, gate_block)),
                  pl.BlockSpec((tm, n), lambda i: (i, gate_block + 1))],
        out_specs=pl.BlockSpec((tm, n), row),
        out_shape=jax.ShapeDtypeStruct((t, n), BF16),
        compiler_params=_params("parallel"),
        name="mix",
    )(oa, ob, wg, ws, main, main)


def _layer(x2, positions, batch, seq, ln1_w, w_in, conv_w, a_log, dt_bias, gdn_norm_w, w_gdn_proj,
           q_norm_w, k_norm_w, w_swa_proj, w_out, ln2_w, w_ff1, w_ff2):
    d_model = x2.shape[1]
    gw3 = 3 * GDN_WIDTH
    o_a, o_b = gw3 + GDN_WIDTH, gw3 + GDN_WIDTH + GDN_HEADS
    o_swa = o_b + GDN_HEADS
    o_ga = o_swa + 3 * SWA_WIDTH
    assert seq % SWA_TOKENS == 0 and seq % GDN_BLOCK == 0
    assert all(window // dl == SWA_BLOCK and SWA_TOKENS % (dl * SWA_BLOCK) == 0 for window, dl in SWA_GROUPS)
    swa_cols = []
    for j in range(SWA_HEADS_PER_GROUP):
        for g in range(len(SWA_GROUPS)):
            for part in range(3):
                lo = o_swa + part * SWA_WIDTH + (g * SWA_HEADS_PER_GROUP + j) * HEAD_DIM
                swa_cols.append(w_in[:, lo:lo + HEAD_DIM])
    conv_src = w_in[:, :gw3]
    pass_src = jnp.concatenate([w_in[:, o_ga:], w_in[:, gw3:o_a]], axis=1)
    cc, pc = conv_src.shape[1] // MAIN_TILES, pass_src.shape[1] // MAIN_TILES
    w_main = jnp.concatenate([part for jt in range(MAIN_TILES)
                              for part in (conv_src[:, jt * cc:(jt + 1) * cc], pass_src[:, jt * pc:(jt + 1) * pc])],
                             axis=1).astype(BF16)
    w_swa = jnp.concatenate(swa_cols, axis=1).astype(BF16)
    w_ab = jnp.pad(w_in[:, o_a:o_swa], ((0, 0), (0, LANES - 2 * GDN_HEADS))).astype(BF16)
    ln1 = ln1_w.reshape(1, -1)
    act, passed, ab, h1 = _main_proj(x2, ln1, w_main, w_ab, conv_w.astype(F32), tm=512, seq=seq, nj=MAIN_TILES)

    pad8 = lambda v: jnp.pad(v.astype(F32), (0, LANES - v.shape[0])).reshape(1, LANES)
    o_a_branch = _gdn(act, passed, 2 * d_model // GDN_WIDTH, ab, pad8(a_log), pad8(dt_bias),
                      gdn_norm_w.reshape(1, -1).astype(F32), batch, seq)

    half = HEAD_DIM // 2
    inv_freq = ROPE_THETA ** (-jnp.arange(half, dtype=F32) / half)
    cos_t, sin_t = _rope_tables(positions.reshape(-1, 1), jnp.concatenate([inv_freq, inv_freq]).reshape(1, HEAD_DIM))
    swa_in = _swa_proj(h1, w_swa, cos_t, sin_t, q_norm_w.reshape(1, -1).astype(F32),
                       k_norm_w.reshape(1, -1).astype(F32), tm=1024, tn=SWA_PROJ_TILE_N)
    o_b_branch = _swa(swa_in, batch, seq)

    mixed = _mix(o_a_branch, o_b_branch, w_gdn_proj.astype(BF16), w_swa_proj.astype(BF16), passed,
                 gate_block=0, tm=512)
    x2 = _residual_matmul(mixed, w_out.astype(BF16), x2, tm=512, tk=d_model, name="out_proj")
    u2 = _norm_matmul(x2, ln2_w.reshape(1, -1), w_ff1.astype(BF16), tm=1024, tn=2048, name="ff1", square_relu=True)
    return _residual_matmul(u2, w_ff2.astype(BF16), x2, tm=1024, tk=1024, name="ff2")


def kernel(x, positions, ln1_w, w_in, gdn_conv_w, gdn_a_log, gdn_dt_bias, gdn_norm_w, w_gdn_proj,
           swa_q_norm_w, swa_k_norm_w, w_swa_proj, w_out, ln2_w, w_ff1, w_ff2):
    batch, seq, d_model = x.shape
    x2 = x.reshape(batch * seq, d_model)
    for l in range(ln1_w.shape[0]):
        x2 = _layer(x2, positions, batch, seq, ln1_w[l], w_in[l], gdn_conv_w[l], gdn_a_log[l], gdn_dt_bias[l],
                    gdn_norm_w[l], w_gdn_proj[l], swa_q_norm_w[l], swa_k_norm_w[l], w_swa_proj[l], w_out[l],
                    ln2_w[l], w_ff1[l], w_ff2[l])
    return x2.reshape(batch, seq, d_model)
```

```python
import functools

import jax
import jax.numpy as jnp
from jax import lax
from jax.experimental import pallas as pl
from jax.experimental.pallas import tpu as pltpu

F32 = jnp.float32
BF16 = jnp.bfloat16

HEAD_DIM = 128
EPS = 1e-6
GDN_HEADS = 8
GDN_WIDTH = GDN_HEADS * HEAD_DIM
GDN_CONV = 4
GDN_CHUNK = 64
SWA_GROUPS = ((128, 1), (512, 4), (2048, 16))
SWA_HEADS_PER_GROUP = 4
SWA_GROUP_WIDTH = SWA_HEADS_PER_GROUP * HEAD_DIM
SWA_WIDTH = len(SWA_GROUPS) * SWA_GROUP_WIDTH
SWA_BLOCK = 128
ROPE_THETA = 10000.0
NEG_INF = -1e30

VMEM_LIMIT_BYTES = 56 * 1024 * 1024
LANES = 128
SUBLANES = 8
CONV_TAIL_ROWS = SUBLANES
NORM_ROWS = 256
GDN_BLOCK = 256
SWA_TOKENS = 2048
SWA_PROJ_TILE_N = 2304


def _params(*sem):
    return pltpu.CompilerParams(dimension_semantics=sem, vmem_limit_bytes=VMEM_LIMIT_BYTES)


def _sigmoid(x):
    return 0.5 * jnp.tanh(0.5 * x) + 0.5


def _silu(x):
    h = 0.5 * x
    return h * jnp.tanh(h) + h


def _rope_table_kernel(pos_ref, invf_ref, cos_ref, sin_ref):
    ang = pos_ref[...].astype(F32) * invf_ref[...]
    cos_ref[...] = jnp.cos(ang)
    lane = lax.broadcasted_iota(jnp.int32, ang.shape, 1)
    s = jnp.sin(ang)
    sin_ref[...] = jnp.where(lane < HEAD_DIM // 2, -s, s)


def _rope_tables(pos_col, inv_freq2):
    t = pos_col.shape[0]
    tm = min(t, 2048)
    return pl.pallas_call(
        _rope_table_kernel,
        grid=(t // tm,),
        in_specs=[pl.BlockSpec((tm, 1), lambda i: (i, 0)),
                  pl.BlockSpec((1, HEAD_DIM), lambda i: (0, 0))],
        out_specs=[pl.BlockSpec((tm, HEAD_DIM), lambda i: (i, 0)),
                   pl.BlockSpec((tm, HEAD_DIM), lambda i: (i, 0))],
        out_shape=[jax.ShapeDtypeStruct((t, HEAD_DIM), F32)] * 2,
        compiler_params=_params("parallel"),
        name="rope_table",
    )(pos_col, inv_freq2)


def _rmsnorm_to(h_ref, x_ref, lnw_ref):
    tm = x_ref.shape[0]
    w = lnw_ref[...]

    def body(c, carry):
        rows = pl.ds(pl.multiple_of(c * NORM_ROWS, NORM_ROWS), NORM_ROWS)
        x = x_ref[rows, :]
        y = x * lax.rsqrt(jnp.mean(x * x, axis=-1, keepdims=True) + EPS)
        h_ref[rows, :] = (y * w).astype(BF16)
        return carry

    lax.fori_loop(0, tm // NORM_ROWS, body, 0)


def _norm_matmul_kernel(x_ref, lnw_ref, w_ref, *rest, with_ab, square_relu):
    if with_ab:
        wab_ref, o_ref, ab_ref, h_ref = rest
    else:
        o_ref, h_ref = rest

    @pl.when(pl.program_id(1) == 0)
    def _():
        _rmsnorm_to(h_ref, x_ref, lnw_ref)
        if with_ab:
            ab_ref[...] = jnp.dot(h_ref[...], wab_ref[...], preferred_element_type=F32)

    acc = jnp.dot(h_ref[...], w_ref[...], preferred_element_type=F32)
    if square_relu:
        acc = jnp.maximum(acc, 0.0)
        acc = acc * acc
    o_ref[...] = acc.astype(o_ref.dtype)


def _norm_matmul(x2, ln_w, w, tm, tn, name, w_ab=None, square_relu=False):
    t, k = x2.shape
    n = w.shape[1]
    in_specs = [pl.BlockSpec((tm, k), lambda i, j: (i, 0)),
                pl.BlockSpec((1, k), lambda i, j: (0, 0)),
                pl.BlockSpec((k, tn), lambda i, j: (0, j))]
    out_specs = [pl.BlockSpec((tm, tn), lambda i, j: (i, j))]
    out_shape = [jax.ShapeDtypeStruct((t, n), BF16)]
    args = [x2, ln_w, w]
    if w_ab is not None:
        in_specs.append(pl.BlockSpec((k, LANES), lambda i, j: (0, 0)))
        out_specs += [pl.BlockSpec((tm, LANES), lambda i, j: (i, 0)), pl.BlockSpec((tm, k), lambda i, j: (i, 0))]
        out_shape += [jax.ShapeDtypeStruct((t, LANES), F32), jax.ShapeDtypeStruct((t, k), BF16)]
        args.append(w_ab)
    out = pl.pallas_call(
        functools.partial(_norm_matmul_kernel, with_ab=w_ab is not None, square_relu=square_relu),
        grid=(t // tm, n // tn),
        in_specs=in_specs, out_specs=out_specs, out_shape=out_shape,
        scratch_shapes=[] if w_ab is not None else [pltpu.VMEM((tm, k), BF16)],
        compiler_params=_params("parallel", "arbitrary"),
        name=name,
    )(*args)
    return out if w_ab is not None else out[0]


def _swa_proj_kernel(h_ref, w_ref, cos_ref, sin_ref, qnw_ref, knw_ref, o_ref, acc_ref):
    d = HEAD_DIM

    @pl.when(pl.program_id(0) == 0)
    def _():
        acc_ref[...] = jnp.zeros(acc_ref.shape, F32)

    cos, sin = cos_ref[...], sin_ref[...]
    norm_w = (qnw_ref[...] * (d ** -0.5), knw_ref[...])
    for g in range(acc_ref.shape[1] // d):
        y = acc_ref[:, g * d:(g + 1) * d]
        if g % 3 < 2:
            y = y * lax.rsqrt(jnp.mean(y * y, axis=-1, keepdims=True) + EPS) * norm_w[g % 3]
            y = y * cos + pltpu.roll(y, d // 2, 1) * sin
        o_ref[:, g * d:(g + 1) * d] = y.astype(o_ref.dtype)
    acc_ref[...] = jnp.dot(h_ref[...], w_ref[...], preferred_element_type=F32)


def _swa_proj(h, w, cos_t, sin_t, qnw, knw, tm, tn):
    t, k = h.shape
    n = w.shape[1]
    nj = n // tn
    n_tiles = (t // tm) * nj
    cur = lambda s: jnp.minimum(s, n_tiles - 1)
    prev = lambda s: jnp.maximum(s - 1, 0)
    const = lambda s: (0, 0)
    return pl.pallas_call(
        _swa_proj_kernel,
        grid=(n_tiles + 1,),
        in_specs=[pl.BlockSpec((tm, k), lambda s: (cur(s) // nj, 0)),
                  pl.BlockSpec((k, tn), lambda s: (0, cur(s) % nj)),
                  pl.BlockSpec((tm, HEAD_DIM), lambda s: (prev(s) // nj, 0)),
                  pl.BlockSpec((tm, HEAD_DIM), lambda s: (prev(s) // nj, 0)),
                  pl.BlockSpec((1, HEAD_DIM), const), pl.BlockSpec((1, HEAD_DIM), const)],
        out_specs=pl.BlockSpec((tm, tn), lambda s: (prev(s) // nj, prev(s) % nj)),
        out_shape=jax.ShapeDtypeStruct((t, n), BF16),
        scratch_shapes=[pltpu.VMEM((tm, tn), F32)],
        compiler_params=_params("arbitrary"),
        name="in_proj_swa",
    )(h, w, cos_t, sin_t, qnw, knw)


def _residual_matmul_kernel(a_ref, w_ref, r_ref, o_ref):
    @pl.when(pl.program_id(1) == 0)
    def _():
        o_ref[...] = r_ref[...]

    o_ref[...] += jnp.dot(a_ref[...], w_ref[...], preferred_element_type=F32)


def _residual_matmul(a, w, resid, tm, tk, name):
    t, k = a.shape
    n = w.shape[1]
    return pl.pallas_call(
        _residual_matmul_kernel,
        grid=(t // tm, k // tk),
        in_specs=[pl.BlockSpec((tm, tk), lambda i, kk: (i, kk)),
                  pl.BlockSpec((tk, n), lambda i, kk: (kk, 0)),
                  pl.BlockSpec((tm, n), lambda i, kk: (i, 0))],
        out_specs=pl.BlockSpec((tm, n), lambda i, kk: (i, 0)),
        out_shape=jax.ShapeDtypeStruct((t, n), F32),
        compiler_params=_params("parallel", "arbitrary"),
        name=name,
    )(a, w, resid)


def _gdn_kernel(qkvz_ref, ab_ref, convw_ref, alog_ref, dtb_ref, normw_ref, o_ref,
                xs_ref, act_ref, kt_ref, g_ref, gt_ref, beta_ref, state_ref):
    nbatch, tb = qkvz_ref.shape[0], qkvz_ref.shape[1]
    c_len, d, nh, w = GDN_CHUNK, HEAD_DIM, GDN_HEADS, GDN_WIDTH
    n_chunks = tb // c_len

    @pl.when(pl.program_id(0) == 0)
    def _():
        xs_ref[:, 0:CONV_TAIL_ROWS, :] = jnp.zeros((nbatch, CONV_TAIL_ROWS, 3 * w), F32)
        state_ref[...] = jnp.zeros(state_ref.shape, F32)

    ri = lax.broadcasted_iota(jnp.int32, (tb, tb), 0)
    ci = lax.broadcasted_iota(jnp.int32, (tb, tb), 1)
    tri = ((ci <= ri) & (ri // c_len == ci // c_len)).astype(F32)

    for b in range(nbatch):
        xs_ref[b, CONV_TAIL_ROWS:CONV_TAIL_ROWS + tb, :] = qkvz_ref[b, :, 0:3 * w].astype(F32)
        for cg in range(3 * nh):
            cols = slice(cg * d, (cg + 1) * d)
            acc = (xs_ref[b, CONV_TAIL_ROWS:CONV_TAIL_ROWS + tb, cols]
                   * convw_ref[GDN_CONV - 1:GDN_CONV, cols])
            for s in range(1, GDN_CONV):
                acc = acc + (xs_ref[b, CONV_TAIL_ROWS - s:CONV_TAIL_ROWS - s + tb, cols]
                             * convw_ref[GDN_CONV - 1 - s:GDN_CONV - s, cols])
            y = _silu(acc)
            if cg < 2 * nh:
                y = y * lax.rsqrt(jnp.sum(y * y, axis=-1, keepdims=True) + EPS)
                if cg < nh:
                    y = y * (d ** -0.5)
            act_ref[b, :, cols] = y
            if nh <= cg < 2 * nh:
                for c in range(n_chunks):
                    kt_ref[b, c, cg - nh] = y[c * c_len:(c + 1) * c_len, :].T
        xs_ref[b, 0:CONV_TAIL_ROWS, :] = xs_ref[b, tb:tb + CONV_TAIL_ROWS, :]

        ab = ab_ref[b]
        beta_ref[b] = _sigmoid(ab)
        z = ab + dtb_ref[...]
        softplus = jnp.maximum(z, 0.0) + jnp.log1p(jnp.exp(-jnp.abs(z)))
        g = -jnp.exp(alog_ref[...]) * softplus
        g_cum = jnp.dot(tri, g, precision=lax.Precision.HIGHEST, preferred_element_type=F32)
        g_ref[b] = g_cum
        for c in range(n_chunks):
            gt_ref[b, c] = g_cum[c * c_len:(c + 1) * c_len, :].T

    ii = lax.broadcasted_iota(jnp.int32, (c_len, c_len), 0)
    jj = lax.broadcasted_iota(jnp.int32, (c_len, c_len), 1)
    incl = ii >= jj
    strict = ii > jj
    norm_w = normw_ref[...]
    chains = [(b, h) for b in range(nbatch) for h in range(nh)]

    def mm(a, b):
        return jnp.dot(a.astype(BF16), b.astype(BF16), preferred_element_type=F32)

    def chunk_body(c, carry):
        rows = slice(c * c_len, (c + 1) * c_len)
        g_all = [g_ref[b, rows, :] for b in range(nbatch)]
        beta_all = [beta_ref[b, rows, :] for b in range(nbatch)]
        gt_all = [gt_ref[b, c] for b in range(nbatch)]

        st = []
        for b, h in chains:
            q = act_ref[b, rows, h * d:(h + 1) * d]
            k = act_ref[b, rows, w + h * d:w + (h + 1) * d]
            v = act_ref[b, rows, 2 * w + h * d:2 * w + (h + 1) * d]
            kt = kt_ref[b, c, h]
            g_col = g_all[b][:, h:h + 1]
            b_col = beta_all[b][:, nh + h:nh + h + 1]
            g_row = gt_all[b][h:h + 1, :]
            g_last = g_row[:, c_len - 1:c_len]
            decay = jnp.where(incl, jnp.exp(jnp.where(incl, g_col - g_row, 0.0)), 0.0)
            e_g = jnp.exp(g_col)
            kb = k * b_col
            st.append(dict(q=q, kt=kt, g_row=g_row, g_last=g_last, decay=decay, kb=kb,
                           qe=q * e_g, rhs=jnp.concatenate([v * b_col, kb * e_g], axis=1)))
        for s in st:
            s1 = mm(jnp.concatenate([s["kb"], s["q"]], axis=0), s["kt"])
            s["a"] = jnp.where(strict, s1[:c_len] * s["decay"], 0.0)
            s["a_qk"] = jnp.where(incl, s1[c_len:] * s["decay"], 0.0)
        for s in st:
            s["x"] = s["rhs"] - mm(s["a"], s["rhs"])
            s["m"] = mm(s["a"], s["a"])
        for it in range(5):
            for s in st:
                s["x"] = s["x"] + mm(s["m"], s["x"])
                if it < 4:
                    s["m"] = mm(s["m"], s["m"])
        for (b, h), s in zip(chains, st):
            s["state"] = state_ref[b, h]
            s["ws"] = mm(jnp.concatenate([s["x"][:, d:], s["qe"]], axis=0), s["state"])
        for (b, h), s in zip(chains, st):
            v_new = s["x"][:, :d] - s["ws"][:c_len]
            o = s["ws"][c_len:] + mm(s["a_qk"], v_new)
            kdt = s["kt"] * jnp.exp(s["g_last"] - s["g_row"])
            state_ref[b, h] = s["state"] * jnp.exp(s["g_last"]) + mm(kdt, v_new)
            o = o * lax.rsqrt(jnp.mean(o * o, axis=-1, keepdims=True) + EPS)
            zh = qkvz_ref[b, rows, 3 * w + h * d:3 * w + (h + 1) * d].astype(F32)
            o_ref[b, rows, h * d:(h + 1) * d] = (o * norm_w * _silu(zh)).astype(o_ref.dtype)
        return carry

    for c in range(n_chunks):
        chunk_body(c, 0)


def _gdn(main, ab, conv_w, a_log, dt_bias, norm_w, batch, seq):
    tb = GDN_BLOCK
    w = GDN_WIDTH
    n_chunks = tb // GDN_CHUNK
    blk = lambda s: (0, s, 0)
    const = lambda s: (0, 0)
    out = pl.pallas_call(
        _gdn_kernel,
        grid=(seq // tb,),
        in_specs=[pl.BlockSpec((batch, tb, 4 * w), blk),
                  pl.BlockSpec((batch, tb, LANES), blk),
                  pl.BlockSpec((GDN_CONV, 3 * w), const),
                  pl.BlockSpec((1, LANES), const),
                  pl.BlockSpec((1, LANES), const),
                  pl.BlockSpec((1, HEAD_DIM), const)],
        out_specs=pl.BlockSpec((batch, tb, w), blk),
        out_shape=jax.ShapeDtypeStruct((batch, seq, w), BF16),
        scratch_shapes=[pltpu.VMEM((batch, tb + CONV_TAIL_ROWS, 3 * w), F32),
                        pltpu.VMEM((batch, tb, 3 * w), F32),
                        pltpu.VMEM((batch, n_chunks, GDN_HEADS, HEAD_DIM, GDN_CHUNK), F32),
                        pltpu.VMEM((batch, tb, LANES), F32),
                        pltpu.VMEM((batch, n_chunks, LANES, GDN_CHUNK), F32),
                        pltpu.VMEM((batch, tb, LANES), F32),
                        pltpu.VMEM((batch, GDN_HEADS, HEAD_DIM, HEAD_DIM), F32)],
        compiler_params=_params("arbitrary"),
        name="gdn",
    )(main.reshape(batch, seq, main.shape[1]), ab.reshape(batch, seq, LANES), conv_w, a_log, dt_bias, norm_w)
    return out.reshape(batch * seq, w)


def _swa_kernel(x_ref, o_ref,
                q_s, k_s0, k_s1, k_s2, v_s0, v_s1, v_s2, og_s, lg_s):
    tq, d, blk = x_ref.shape[0], HEAD_DIM, SWA_BLOCK
    first = pl.program_id(2) == 0
    k_ss, v_ss = (k_s0, k_s1, k_s2), (v_s0, v_s1, v_s2)
    qi = lax.broadcasted_iota(jnp.int32, (blk, 2 * blk), 0)
    cj = lax.broadcasted_iota(jnp.int32, (blk, 2 * blk), 1)
    band = (cj >= qi) & (cj <= qi + blk)
    band_first = band & (cj >= jnp.where(first, blk, 0))
    ones_v = jnp.ones((2 * blk, d), BF16)

    for g, (window, dl) in enumerate(SWA_GROUPS):
        tail = blk * dl
        k_s, v_s = k_ss[g], v_ss[g]

        @pl.when(first)
        def _(k_s=k_s, v_s=v_s, tail=tail):
            k_s[0:tail, :] = jnp.zeros((tail, d), F32)
            v_s[0:tail, :] = jnp.zeros((tail, d), F32)

        base = 3 * g * d
        q_s[...] = x_ref[:, base:base + d].astype(F32)
        k_s[tail:tail + tq, :] = x_ref[:, base + d:base + 2 * d].astype(F32)
        v_s[tail:tail + tq, :] = x_ref[:, base + 2 * d:base + 3 * d].astype(F32)
        for r in range(dl):
            for i in range(tq // tail):
                start = tail * i + r
                if dl == 1:
                    qrows, krows = pl.ds(start, blk), pl.ds(start, 2 * blk)
                else:
                    qrows, krows = pl.ds(start, blk, stride=dl), pl.ds(start, 2 * blk, stride=dl)
                q = q_s[qrows, :].astype(BF16)
                k = k_s[krows, :].astype(BF16)
                v = jnp.concatenate([v_s[krows, :].astype(BF16), ones_v], axis=1)
                s = lax.dot_general(q, k, (((1,), (1,)), ((), ())), preferred_element_type=F32)
                s = jnp.where(band_first if i == 0 else band, s, NEG_INF)
                mx = jnp.max(s, axis=-1, keepdims=True)
                p = jnp.exp(s - mx)
                pv = jnp.dot(p.astype(BF16), v, preferred_element_type=F32)
                den = pv[:, d:]
                og_s[g, qrows, :] = pv[:, :d] / den
                lg_s[g, qrows, :] = mx + jnp.log(den)
        k_s[0:tail, :] = k_s[tq:tq + tail, :]
        v_s[0:tail, :] = v_s[tq:tq + tail, :]

    l0, l1, l2 = lg_s[0], lg_s[1], lg_s[2]
    mx = jnp.maximum(jnp.maximum(l0, l1), l2)
    e0, e1, e2 = jnp.exp(l0 - mx), jnp.exp(l1 - mx), jnp.exp(l2 - mx)
    o_ref[...] = ((og_s[0] * e0 + og_s[1] * e1 + og_s[2] * e2) / (e0 + e1 + e2)).astype(o_ref.dtype)


def _swa(xs, batch, seq):
    t = xs.shape[0]
    d = HEAD_DIM
    tq = SWA_TOKENS
    nb = seq // tq
    slot = lambda b, j, n: (b * nb + n, j)
    dils = [dl for _, dl in SWA_GROUPS]
    return pl.pallas_call(
        _swa_kernel,
        grid=(batch, SWA_HEADS_PER_GROUP, nb),
        in_specs=[pl.BlockSpec((tq, 3 * len(SWA_GROUPS) * d), slot)],
        out_specs=pl.BlockSpec((tq, d), slot),
        out_shape=jax.ShapeDtypeStruct((t, SWA_GROUP_WIDTH), BF16),
        scratch_shapes=[pltpu.VMEM((tq, d), F32)]
        + [pltpu.VMEM((SWA_BLOCK * dl + tq, d), F32) for dl in dils] * 2
        + [pltpu.VMEM((len(SWA_GROUPS), tq, d), F32)] * 2,
        compiler_params=_params("parallel", "parallel", "arbitrary"),
        name="swa",
    )(xs)


def _mix_kernel(oa_ref, ob_ref, wg_ref, ws_ref, ga_ref, gb_ref, out_ref):
    ya = jnp.dot(oa_ref[...], wg_ref[...], preferred_element_type=F32)
    yb = jnp.dot(ob_ref[...], ws_ref[...], preferred_element_type=F32)
    mixed = _sigmoid(ga_ref[...].astype(F32)) * ya + _sigmoid(gb_ref[...].astype(F32)) * yb
    out_ref[...] = mixed.astype(out_ref.dtype)


def _mix(oa, ob, wg, ws, main, gate_block, tm):
    t = oa.shape[0]
    n = wg.shape[1]
    row = lambda i: (i, 0)
    const = lambda i: (0, 0)
    return pl.pallas_call(
        _mix_kernel,
        grid=(t // tm,),
        in_specs=[pl.BlockSpec((tm, oa.shape[1]), row), pl.BlockSpec((tm, ob.shape[1]), row),
                  pl.BlockSpec(wg.shape, const), pl.BlockSpec(ws.shape, const),
                  pl.BlockSpec((tm, n), lambda i: (i, gate_block)),
                  pl.BlockSpec((tm, n), lambda i: (i, gate_block + 1))],
        out_specs=pl.BlockSpec((tm, n), row),
        out_shape=jax.ShapeDtypeStruct((t, n), BF16),
        compiler_params=_params("parallel"),
        name="mix",
    )(oa, ob, wg, ws, main, main)


def _mix_out_kernel(oa_ref, ob_ref, wg_ref, ws_ref, ga_ref, gb_ref, wo_ref, r_ref, out_ref):
    ya = jnp.dot(oa_ref[...], wg_ref[...], preferred_element_type=F32)
    yb = jnp.dot(ob_ref[...], ws_ref[...], preferred_element_type=F32)
    mixed = _sigmoid(ga_ref[...].astype(F32)) * ya + _sigmoid(gb_ref[...].astype(F32)) * yb
    out_ref[...] = r_ref[...] + jnp.dot(mixed.astype(BF16), wo_ref[...], preferred_element_type=F32)


def _mix_out(oa, ob, wg, ws, main, gate_block, wo, resid, tm):
    t = oa.shape[0]
    n = wg.shape[1]
    row = lambda i: (i, 0)
    const = lambda i: (0, 0)
    once = dict(pipeline_mode=pl.Buffered(1))
    return pl.pallas_call(
        _mix_out_kernel,
        grid=(t // tm,),
        in_specs=[pl.BlockSpec((tm, oa.shape[1]), row), pl.BlockSpec((tm, ob.shape[1]), row),
                  pl.BlockSpec(wg.shape, const, **once), pl.BlockSpec(ws.shape, const, **once),
                  pl.BlockSpec((tm, n), lambda i: (i, gate_block)),
                  pl.BlockSpec((tm, n), lambda i: (i, gate_block + 1)),
                  pl.BlockSpec(wo.shape, const, **once),
                  pl.BlockSpec((tm, wo.shape[1]), row)],
        out_specs=pl.BlockSpec((tm, wo.shape[1]), row),
        out_shape=jax.ShapeDtypeStruct((t, wo.shape[1]), F32),
        compiler_params=_params("parallel"),
        name="mix_out",
    )(oa, ob, wg, ws, main, main, wo, resid)


def _layer(x2, positions, batch, seq, ln1_w, w_in, conv_w, a_log, dt_bias, gdn_norm_w, w_gdn_proj,
           q_norm_w, k_norm_w, w_swa_proj, w_out, ln2_w, w_ff1, w_ff2):
    d_model = x2.shape[1]
    gw3 = 3 * GDN_WIDTH
    o_a, o_b = gw3 + GDN_WIDTH, gw3 + GDN_WIDTH + GDN_HEADS
    o_swa = o_b + GDN_HEADS
    o_ga = o_swa + 3 * SWA_WIDTH
    assert seq % SWA_TOKENS == 0 and seq % GDN_BLOCK == 0
    assert all(window // dl == SWA_BLOCK and SWA_TOKENS % (dl * SWA_BLOCK) == 0 for window, dl in SWA_GROUPS)
    swa_cols = []
    for j in range(SWA_HEADS_PER_GROUP):
        for g in range(len(SWA_GROUPS)):
            for part in range(3):
                lo = o_swa + part * SWA_WIDTH + (g * SWA_HEADS_PER_GROUP + j) * HEAD_DIM
                swa_cols.append(w_in[:, lo:lo + HEAD_DIM])
    w_main = jnp.concatenate([w_in[:, :o_a], w_in[:, o_ga:]], axis=1).astype(BF16)
    w_swa = jnp.concatenate(swa_cols, axis=1).astype(BF16)
    w_ab = jnp.pad(w_in[:, o_a:o_swa], ((0, 0), (0, LANES - 2 * GDN_HEADS))).astype(BF16)
    ln1 = ln1_w.reshape(1, -1)
    main, ab, h1 = _norm_matmul(x2, ln1, w_main, tm=1024, tn=2048, name="in_proj_main", w_ab=w_ab)

    pad8 = lambda v: jnp.pad(v.astype(F32), (0, LANES - v.shape[0])).reshape(1, LANES)
    o_a_branch = _gdn(main, ab, conv_w.astype(F32), pad8(a_log), pad8(dt_bias),
                      gdn_norm_w.reshape(1, -1).astype(F32), batch, seq)

    half = HEAD_DIM // 2
    inv_freq = ROPE_THETA ** (-jnp.arange(half, dtype=F32) / half)
    cos_t, sin_t = _rope_tables(positions.reshape(-1, 1), jnp.concatenate([inv_freq, inv_freq]).reshape(1, HEAD_DIM))
    swa_in = _swa_proj(h1, w_swa, cos_t, sin_t, q_norm_w.reshape(1, -1).astype(F32),
                       k_norm_w.reshape(1, -1).astype(F32), tm=1024, tn=SWA_PROJ_TILE_N)
    o_b_branch = _swa(swa_in, batch, seq)

    x2 = _mix_out(o_a_branch, o_b_branch, w_gdn_proj.astype(BF16), w_swa_proj.astype(BF16), main,
                  4 * GDN_WIDTH // d_model, w_out.astype(BF16), x2, tm=512)
    u2 = _norm_matmul(x2, ln2_w.reshape(1, -1), w_ff1.astype(BF16), tm=1024, tn=2048, name="ff1", square_relu=True)
    return _residual_matmul(u2, w_ff2.astype(BF16), x2, tm=1024, tk=1024, name="ff2")


def kernel(x, positions, ln1_w, w_in, gdn_conv_w, gdn_a_log, gdn_dt_bias, gdn_norm_w, w_gdn_proj,
           swa_q_norm_w, swa_k_norm_w, w_swa_proj, w_out, ln2_w, w_ff1, w_ff2):
    batch, seq, d_model = x.shape
    x2 = x.reshape(batch * seq, d_model)
    for l in range(ln1_w.shape[0]):
        x2 = _layer(x2, positions, batch, seq, ln1_w[l], w_in[l], gdn_conv_w[l], gdn_a_log[l], gdn_dt_bias[l],
                    gdn_norm_w[l], w_gdn_proj[l], swa_q_norm_w[l], swa_k_norm_w[l], w_swa_proj[l], w_out[l],
                    ln2_w[l], w_ff1[l], w_ff2[l])
    return x2.reshape(batch, seq, d_model)
```
